```python
import math
import jax, jax.numpy as jnp
from jax import lax
import numpy as np

D_MODEL = 1024
BATCH = 8
SEQ = 4096
DEPTH = 1

HEAD_DIM = 64
POOL_WIDTH = D_MODEL // 4
POOL_WINDOWS = (2, 4, 8, 16)
POOL_GROUPS = len(POOL_WINDOWS)
POOL_GROUP_DIM = POOL_WIDTH // POOL_GROUPS
ATTN_WIDTH = D_MODEL - POOL_WIDTH
ATTN_HEADS = ATTN_WIDTH // HEAD_DIM
DILATION_CFG = ((128, 1), (512, 4), (2048, 16))
N_DIL = len(DILATION_CFG)
HEADS_PER_DIL = ATTN_HEADS // N_DIL
ATTN_OUT_WIDTH = HEADS_PER_DIL * HEAD_DIM
IN_PROJ_WIDTH = POOL_WIDTH + 3 * ATTN_WIDTH
OUT_PROJ_WIDTH = POOL_WIDTH + ATTN_OUT_WIDTH
ROT_DIM = HEAD_DIM // 4
ROPE_THETA = 500000.0
D_FF = 11 * D_MODEL // 4
CONV_WIDTH = 3
BLOCK = 128
NORM_EPS = 1e-6
N_MOD = 6

kernel_name = "hybrid_pool_dilated_attn_convffn_block"


def rms_norm(x, g):
    x32 = x.astype(jnp.float32)
    y = x32 * lax.rsqrt(jnp.mean(x32 * x32, axis=-1, keepdims=True) + NORM_EPS)
    return (y * g.astype(jnp.float32)).astype(x.dtype)


def partial_rope(t, cos, sin):
    half = ROT_DIM // 2
    t1 = t[..., :half]
    t2 = t[..., half:ROT_DIM]
    return jnp.concatenate(
        [t1 * cos - t2 * sin, t2 * cos + t1 * sin, t[..., ROT_DIM:]], axis=-1)


def causal_multiscale_pool(u, w_pool, b_pool, pool_scale):
    B, S, _ = u.shape
    u32 = u.astype(jnp.float32).reshape(B, S, POOL_GROUPS, POOL_GROUP_DIM)
    cs = jnp.cumsum(u32, axis=1)
    t = jnp.arange(S, dtype=jnp.float32)
    means = []
    for gi, w in enumerate(POOL_WINDOWS):
        csg = cs[:, :, gi]
        lagged = jnp.pad(csg[:, :S - w], ((0, 0), (w, 0), (0, 0)))
        count = jnp.minimum(t + 1.0, float(w))
        means.append((csg - lagged) / count[None, :, None])
    mixed = jnp.stack(means, axis=2) - u32
    y = jnp.einsum('bsgc,gcd->bsgd', mixed, w_pool.astype(jnp.float32)) + b_pool.astype(jnp.float32)
    return (y.reshape(B, S, POOL_WIDTH) * pool_scale.astype(jnp.float32)).astype(u.dtype)


def dilated_window_attention(q, k, v, window, dilation):
    B, S, H, hd = q.shape
    span = window // dilation
    assert span <= BLOCK
    chunk = dilation * BLOCK
    s_pad = -(-S // chunk) * chunk
    padw = ((0, 0), (0, s_pad - S), (0, 0), (0, 0))
    q, k, v = (jnp.pad(a.astype(jnp.float32), padw) for a in (q, k, v))
    nb = s_pad // chunk
    qb, kb, vb = (a.reshape(B, nb, BLOCK, dilation, H, hd) for a in (q, k, v))

    def with_prev(a):
        prev = jnp.pad(a[:, :-1], ((0, 0), (1, 0), (0, 0), (0, 0), (0, 0), (0, 0)))
        return jnp.concatenate([prev, a], axis=2)

    kk, vv = with_prev(kb), with_prev(vb)
    scores = jnp.einsum('bnirhd,bnjrhd->bnrhij', qb, kk)
    i = jnp.arange(BLOCK)[:, None]
    j = jnp.arange(2 * BLOCK)[None, :]
    n = jnp.arange(nb)[:, None, None]
    dist = BLOCK + i - j
    valid = (dist >= 0) & (dist <= span) & (n * BLOCK + j - BLOCK >= 0)
    scores = jnp.where(valid[None, :, None, None], scores, -jnp.inf)
    m = jnp.max(scores, axis=-1, keepdims=True)
    p = jnp.exp(scores - m)
    den = jnp.sum(p, axis=-1)
    o = jnp.einsum('bnrhij,bnjrhd->bnirhd', p, vv)
    den_t = jnp.moveaxis(den, -1, 2)
    lse_t = jnp.moveaxis(m[..., 0] + jnp.log(den), -1, 2)
    o = (o / den_t[..., None]).reshape(B, s_pad, H, hd)[:, :S]
    lse = lse_t.reshape(B, s_pad, H)[:, :S]
    return o, lse


def mixing_sublayer(h, cos, sin, w_in, w_pool, b_pool, pool_scale, w_out):
    B, S, _ = h.shape
    proj = h @ w_in
    u_pool = proj[..., :POOL_WIDTH]
    qkv = proj[..., POOL_WIDTH:].reshape(B, S, 3, ATTN_HEADS, HEAD_DIM)
    q = partial_rope(qkv[:, :, 0], cos, sin) * (HEAD_DIM ** -0.5)
    k = partial_rope(qkv[:, :, 1], cos, sin)
    v = qkv[:, :, 2]
    outs, lses = [], []
    for g, (window, dilation) in enumerate(DILATION_CFG):
        sl = slice(g * HEADS_PER_DIL, (g + 1) * HEADS_PER_DIL)
        o, l = dilated_window_attention(q[:, :, sl], k[:, :, sl], v[:, :, sl], window, dilation)
        outs.append(o)
        lses.append(l)
    alpha = jax.nn.softmax(jnp.stack(lses, axis=0), axis=0)
    attn = jnp.sum(alpha[..., None] * jnp.stack(outs, axis=0), axis=0)
    attn = attn.reshape(B, S, ATTN_OUT_WIDTH).astype(h.dtype)
    pool = causal_multiscale_pool(u_pool, w_pool, b_pool, pool_scale)
    return jnp.concatenate([pool, attn], axis=-1) @ w_out


def conv_ffn(h, w_up, conv_w, conv_b, w_down):
    S = h.shape[1]
    up = h @ w_up
    gate, val = up[..., :D_FF], up[..., D_FF:]
    gp = jnp.pad(gate, ((0, 0), (CONV_WIDTH - 1, 0), (0, 0)))
    gate = gp[:, 0:S] * conv_w[0] + gp[:, 1:S + 1] * conv_w[1] + gp[:, 2:S + 2] * conv_w[2] + conv_b
    return (jax.nn.gelu(gate, approximate=True) * val) @ w_down


def setup_inputs(seed: int = 0) -> dict:
    key = jax.random.key(seed)
    ks = jax.random.split(key, 20)
    f32 = jnp.float32
    nrm = lambda k, shape, s: jax.random.normal(k, shape, f32) * s
    L = DEPTH
    return {
        "x": nrm(ks[0], (BATCH, SEQ, D_MODEL), 1.0),
        "c": nrm(ks[1], (BATCH, D_MODEL), 1.0),
        "positions": jnp.broadcast_to(jnp.arange(SEQ, dtype=jnp.int32), (BATCH, SEQ)),
        "w_ada": nrm(ks[2], (L, D_MODEL, N_MOD * D_MODEL), D_MODEL ** -0.5),
        "b_ada": nrm(ks[3], (L, N_MOD * D_MODEL), 0.02),
        "g_pre_mix": 1.0 + nrm(ks[4], (L, D_MODEL), 0.05),
        "g_post_mix": 1.0 + nrm(ks[5], (L, D_MODEL), 0.05),
        "g_pre_ffn": 1.0 + nrm(ks[6], (L, D_MODEL), 0.05),
        "g_post_ffn": 1.0 + nrm(ks[7], (L, D_MODEL), 0.05),
        "w_in": nrm(ks[8], (L, D_MODEL, IN_PROJ_WIDTH), D_MODEL ** -0.5),
        "w_pool": nrm(ks[9], (L, POOL_GROUPS, POOL_GROUP_DIM, POOL_GROUP_DIM), POOL_GROUP_DIM ** -0.5),
        "b_pool": nrm(ks[10], (L, POOL_GROUPS, POOL_GROUP_DIM), 0.02),
        "pool_scale": 1.0 + nrm(ks[11], (L, POOL_WIDTH), 0.05),
        "w_out": nrm(ks[12], (L, OUT_PROJ_WIDTH, D_MODEL), OUT_PROJ_WIDTH ** -0.5),
        "w_up": nrm(ks[13], (L, D_MODEL, 2 * D_FF), D_MODEL ** -0.5),
        "conv_w": nrm(ks[14], (L, CONV_WIDTH, D_FF), CONV_WIDTH ** -0.5),
        "conv_b": nrm(ks[15], (L, D_FF), 0.02),
        "w_down": nrm(ks[16], (L, D_FF, D_MODEL), D_FF ** -0.5),
    }


def reference(x, c, positions, w_ada, b_ada, g_pre_mix, g_post_mix, g_pre_ffn, g_post_ffn,
              w_in, w_pool, b_pool, pool_scale, w_out, w_up, conv_w, conv_b, w_down):
    inv_freq = ROPE_THETA ** (-jnp.arange(0, ROT_DIM, 2, dtype=jnp.float32) / ROT_DIM)
    ang = positions.astype(jnp.float32)[..., None] * inv_freq
    cos = jnp.cos(ang)[:, :, None, :].astype(x.dtype)
    sin = jnp.sin(ang)[:, :, None, :].astype(x.dtype)
    c_act = jax.nn.silu(c)
    for l in range(DEPTH):
        mod = c_act @ w_ada[l] + b_ada[l]
        sh_m, sc_m, gt_m, sh_f, sc_f, gt_f = (t[:, None, :] for t in jnp.split(mod, N_MOD, axis=-1))
        h = rms_norm(x, g_pre_mix[l]) * (1.0 + sc_m) + sh_m
        y = mixing_sublayer(h, cos, sin, w_in[l], w_pool[l], b_pool[l], pool_scale[l], w_out[l])
        x = x + gt_m * rms_norm(y, g_post_mix[l])
        h = rms_norm(x, g_pre_ffn[l]) * (1.0 + sc_f) + sh_f
        y = conv_ffn(h, w_up[l], conv_w[l], conv_b[l], w_down[l])
        x = x + gt_f * rms_norm(y, g_post_ffn[l])
    return x
```

```python
import functools
import math

import jax
import jax.numpy as jnp
from jax import lax
from jax.experimental import pallas as pl
from jax.experimental.pallas import tpu as pltpu

F32 = jnp.float32
BF16 = jnp.bfloat16

D_MODEL = 1024
HEAD_DIM = 64
POOL_WIDTH = 256
POOL_WINDOWS = (2, 4, 8, 16)
POOL_GROUP_DIM = 64
MAX_POOL_WINDOW = 16
ATTN_WIDTH = 768
GROUP_WIDTH = 256
HEADS_PER_GROUP = 4
DILATIONS = (1, 4, 16)
SPAN = 128
BLOCK = 128
IN_PROJ_WIDTH = 2560
ROT_DIM = 16
ROT_HALF = 8
ROPE_THETA = 500000.0
D_FF = 2816
NORM_EPS = 1e-6
N_MOD = 6
MASK_VALUE = -1e30

LANES = 128
VMEM_LIMIT_BYTES = 56 * 1024 * 1024

ROW_TILE = 512
FF_CHUNK = 256
ATTN_TILES_PER_STEP = 8


def _params(n_axes, vmem=VMEM_LIMIT_BYTES):
    return pltpu.CompilerParams(dimension_semantics=("arbitrary",) * n_axes, vmem_limit_bytes=vmem)


def _rms_scale(v):
    return lax.rsqrt(jnp.mean(v * v, axis=-1, keepdims=True) + NORM_EPS)


def _mod_kernel(c_ref, w_ref, b_ref, o_ref):
    c = c_ref[...]
    c_act = c * jax.nn.sigmoid(c)
    o_ref[...] = jnp.dot(c_act.astype(BF16), w_ref[...].astype(BF16),
                         preferred_element_type=F32) + b_ref[...]


def _mod_call(c, w_ada, b_ada):
    batch = c.shape[0]
    n = w_ada.shape[1]
    tn = 768
    return pl.pallas_call(
        _mod_kernel,
        grid=(n // tn,),
        in_specs=[pl.BlockSpec((batch, D_MODEL), lambda j: (0, 0)),
                  pl.BlockSpec((D_MODEL, tn), lambda j: (0, j)),
                  pl.BlockSpec((1, tn), lambda j: (0, j))],
        out_specs=pl.BlockSpec((batch, tn), lambda j: (0, j)),
        out_shape=jax.ShapeDtypeStruct((batch, n), F32),
        compiler_params=_params(1),
        name="mod",
    )(c, w_ada, b_ada.reshape(1, n))


def _rope_kernel(pos_ref, freq_ref, cos_ref, sin_ref):
    ang = pos_ref[...] * freq_ref[...]
    cos_ref[...] = jnp.cos(ang)
    sin_ref[...] = jnp.sin(ang)


def _rope_tables(positions):
    batch, seq = positions.shape
    inv_freq = ROPE_THETA ** (-jnp.arange(0, ROT_DIM, 2, dtype=F32) / ROT_DIM)
    rows = batch * seq * ROT_HALF // LANES
    pos = jnp.repeat(positions.astype(F32).reshape(-1), ROT_HALF).reshape(rows, LANES)
    freq = jnp.tile(inv_freq, LANES // ROT_HALF).reshape(1, LANES)
    cos, sin = pl.pallas_call(
        _rope_kernel,
        out_shape=[jax.ShapeDtypeStruct((rows, LANES), F32)] * 2,
        name="rope",
    )(pos, freq)
    cos = cos.reshape(batch, seq, ROT_HALF)
    sin = sin.reshape(batch, seq, ROT_HALF)
    ones = jnp.ones((batch, seq, HEAD_DIM - ROT_DIM), F32)
    cos_head = jnp.concatenate([cos, cos, ones], axis=-1)
    sin_head = jnp.concatenate([-sin, sin, 0.0 * ones], axis=-1)
    return (jnp.concatenate([cos_head, cos_head], axis=-1),
            jnp.concatenate([sin_head, sin_head], axis=-1))


def _inproj_kernel(x_ref, mod_ref, g_ref, w_ref, cos_ref, sin_ref, wpool_ref, bpool_ref, pscale_ref,
                   pool_ref, q0_ref, q1_ref, q2_ref, k0_ref, k1_ref, k2_ref, v0_ref, v1_ref, v2_ref,
                   carry_ref):
    s = pl.program_id(1)
    tm = x_ref.shape[1]
    x = x_ref[0]
    shift = mod_ref[0, 0:1, :]
    scale = mod_ref[0, 1:2, :]
    h = x * _rms_scale(x) * g_ref[...] * (1.0 + scale) + shift
    proj = jnp.dot(h.astype(BF16), w_ref[...], preferred_element_type=F32)

    cos = cos_ref[0]
    sin = sin_ref[0]
    lane = lax.broadcasted_iota(jnp.int32, (tm, LANES), 1)
    first_half = (lane % HEAD_DIM) < ROT_HALF

    def rope(t):
        partner = jnp.where(first_half, pltpu.roll(t, LANES - ROT_HALF, 1), pltpu.roll(t, ROT_HALF, 1))
        return t * cos + partner * sin

    q_refs = (q0_ref, q1_ref, q2_ref)
    k_refs = (k0_ref, k1_ref, k2_ref)
    v_refs = (v0_ref, v1_ref, v2_ref)
    q_scale = HEAD_DIM ** -0.5
    for g in range(len(DILATIONS)):
        for half in range(GROUP_WIDTH // LANES):
            lo = g * GROUP_WIDTH + half * LANES
            cols = slice(half * LANES, (half + 1) * LANES)
            q = proj[:, POOL_WIDTH + lo:POOL_WIDTH + lo + LANES]
            k = proj[:, POOL_WIDTH + ATTN_WIDTH + lo:POOL_WIDTH + ATTN_WIDTH + lo + LANES]
            q_refs[g][0, :, cols] = (rope(q) * q_scale).astype(BF16)
            k_refs[g][0, :, cols] = rope(k).astype(BF16)
        lo = POOL_WIDTH + 2 * ATTN_WIDTH + g * GROUP_WIDTH
        v_refs[g][0] = proj[:, lo:lo + GROUP_WIDTH].astype(BF16)

    @pl.when(s == 0)
    def _():
        carry_ref[...] = jnp.zeros_like(carry_ref)

    u = proj[:, :POOL_WIDTH]
    ext = jnp.concatenate([carry_ref[...], u], axis=0)
    carry_ref[...] = u[tm - MAX_POOL_WINDOW:, :]
    sums = []
    acc = ext
    width = 1
    for w in POOL_WINDOWS:
        while width < w:
            acc = acc + pltpu.roll(acc, width, 0)
            width *= 2
        sums.append(acc[MAX_POOL_WINDOW:, :])
    lane_p = lax.broadcasted_iota(jnp.int32, (tm, POOL_WIDTH), 1)
    row_p = lax.broadcasted_iota(jnp.int32, (tm, POOL_WIDTH), 0) + s * tm
    group = lane_p // POOL_GROUP_DIM
    win = sums[-1]
    window = jnp.full((tm, POOL_WIDTH), POOL_WINDOWS[-1], jnp.int32)
    for gi in range(len(POOL_WINDOWS) - 2, -1, -1):
        win = jnp.where(group == gi, sums[gi], win)
        window = jnp.where(group == gi, POOL_WINDOWS[gi], window)
    count = jnp.minimum(row_p + 1, window).astype(F32)
    mixed = win / count - u
    y = jnp.dot(mixed.astype(BF16), wpool_ref[...], preferred_element_type=F32) + bpool_ref[...]
    pool_ref[0] = (y * pscale_ref[...]).astype(BF16)


def _inproj_call(x, mod3, g_pre, w_in, cos_t, sin_t, wpool_blk, bpool, pscale):
    batch, seq, _ = x.shape
    tm = ROW_TILE
    row = lambda b, s: (b, s, 0)
    const = lambda b, s: (0, 0)
    grp = jax.ShapeDtypeStruct((batch, seq, GROUP_WIDTH), BF16)
    return pl.pallas_call(
        _inproj_kernel,
        grid=(batch, seq // tm),
        in_specs=[pl.BlockSpec((1, tm, D_MODEL), row),
                  pl.BlockSpec((1, N_MOD, D_MODEL), lambda b, s: (b, 0, 0)),
                  pl.BlockSpec((1, D_MODEL), const),
                  pl.BlockSpec((D_MODEL, IN_PROJ_WIDTH), const),
                  pl.BlockSpec((1, tm, LANES), row),
                  pl.BlockSpec((1, tm, LANES), row),
                  pl.BlockSpec((POOL_WIDTH, POOL_WIDTH), const),
                  pl.BlockSpec((1, POOL_WIDTH), const),
                  pl.BlockSpec((1, POOL_WIDTH), const)],
        out_specs=[pl.BlockSpec((1, tm, GROUP_WIDTH), row)] * 10,
        out_shape=[grp] * 10,
        scratch_shapes=[pltpu.VMEM((MAX_POOL_WINDOW, POOL_WIDTH), F32)],
        compiler_params=_params(2),
        name="inproj",
    )(x, mod3, g_pre, w_in, cos_t, sin_t, wpool_blk, bpool, pscale)


def _attn_kernel(q_ref, kp_ref, kc_ref, vp_ref, vc_ref, o_ref, lse_ref, *, n_blocks, n_res, first_fn):
    first_step = first_fn(pl.program_id(1))
    i = lax.broadcasted_iota(jnp.int32, (BLOCK, 2 * BLOCK), 0)
    j = lax.broadcasted_iota(jnp.int32, (BLOCK, 2 * BLOCK), 1)
    valid = (j >= i) & (j <= i + SPAN)
    bias = jnp.where(valid, 0.0, MASK_VALUE)
    bias_first = jnp.where(valid & (j >= BLOCK), 0.0, MASK_VALUE)
    lane = lax.broadcasted_iota(jnp.int32, (BLOCK, GROUP_WIDTH), 1)
    head_masks = [(lane // HEAD_DIM) == h for h in range(HEADS_PER_GROUP)]

    for r in range(n_res):
        cols = slice(r * GROUP_WIDTH, (r + 1) * GROUP_WIDTH)
        for n in range(n_blocks):
            rows = slice(n * BLOCK, (n + 1) * BLOCK)
            q = q_ref[0, rows, cols]
            if n == 0:
                k_prev, v_prev = kp_ref[0, :, cols], vp_ref[0, :, cols]
                tile_bias = jnp.where(first_step, bias_first, bias)
            else:
                prev_rows = slice((n - 1) * BLOCK, n * BLOCK)
                k_prev, v_prev = kc_ref[0, prev_rows, cols], vc_ref[0, prev_rows, cols]
                tile_bias = bias
            keys = jnp.concatenate([k_prev, kc_ref[0, rows, cols]], axis=0)
            vals = jnp.concatenate([v_prev, vc_ref[0, rows, cols]], axis=0)
            q_heads = jnp.concatenate([jnp.where(m, q, jnp.zeros_like(q)) for m in head_masks], axis=0)
            scores = lax.dot_general(q_heads, keys, (((1,), (1,)), ((), ())),
                                     preferred_element_type=F32)
            scores = scores + jnp.concatenate([tile_bias] * HEADS_PER_GROUP, axis=0)
            m_row = jnp.max(scores, axis=-1, keepdims=True)
            p = jnp.exp(scores - m_row)
            den = jnp.sum(p, axis=-1, keepdims=True)
            pv = jnp.dot(p.astype(BF16), vals, preferred_element_type=F32)
            pv = pv / den
            lse = m_row + jnp.log(den)
            out = jnp.zeros((BLOCK, GROUP_WIDTH), F32)
            lse_out = jnp.zeros((BLOCK, GROUP_WIDTH), F32)
            for h, m in enumerate(head_masks):
                hrows = slice(h * BLOCK, (h + 1) * BLOCK)
                out = jnp.where(m, pv[hrows], out)
                lse_out = jnp.where(m, lse[hrows], lse_out)
            o_ref[0, rows, cols] = out.astype(BF16)
            lse_ref[0, rows, cols] = lse_out


def _attn_call(q, k, v, dilation):
    batch, seq, _ = q.shape
    d = dilation
    rows_total = seq // d
    width = d * GROUP_WIDTH
    n_res = min(d, ATTN_TILES_PER_STEP)
    n_blocks = ATTN_TILES_PER_STEP // n_res
    col_steps = d // n_res
    row_steps = rows_total // (n_blocks * BLOCK)
    view = lambda a: a.reshape(batch, rows_total, width)

    def cur(b, t):
        return (b, t // col_steps, t % col_steps)

    def prev(b, t):
        return (b, jnp.maximum((t // col_steps) * n_blocks - 1, 0), t % col_steps)

    cur_spec = pl.BlockSpec((1, n_blocks * BLOCK, n_res * GROUP_WIDTH), cur)
    prev_spec = pl.BlockSpec((1, BLOCK, n_res * GROUP_WIDTH), prev)
    kernel = functools.partial(_attn_kernel, n_blocks=n_blocks, n_res=n_res,
                               first_fn=lambda t: t // col_steps == 0)
    o, lse = pl.pallas_call(
        kernel,
        grid=(batch, row_steps * col_steps),
        in_specs=[cur_spec, prev_spec, cur_spec, prev_spec, cur_spec],
        out_specs=[cur_spec, cur_spec],
        out_shape=[jax.ShapeDtypeStruct((batch, rows_total, width), BF16),
                   jax.ShapeDtypeStruct((batch, rows_total, width), F32)],
        compiler_params=_params(2),
        name=f"attn_d{d}",
    )(view(q), view(k), view(k), view(v), view(v))
    return o.reshape(batch, seq, GROUP_WIDTH), lse.reshape(batch, seq, GROUP_WIDTH)


def _outproj_kernel(x_ref, mod_ref, gpost_ref, gpre_ref, w_ref, pool_ref,
                    o0_ref, o1_ref, o2_ref, l0_ref, l1_ref, l2_ref, x1_ref, h2_ref):
    lses = [l0_ref[0], l1_ref[0], l2_ref[0]]
    outs = [o0_ref[0], o1_ref[0], o2_ref[0]]
    top = jnp.maximum(jnp.maximum(lses[0], lses[1]), lses[2])
    weights = [jnp.exp(l - top) for l in lses]
    total = weights[0] + weights[1] + weights[2]
    attn = sum((w / total) * o.astype(F32) for w, o in zip(weights, outs))
    mixed = jnp.concatenate([pool_ref[0], attn.astype(BF16)], axis=-1)
    y = jnp.dot(mixed, w_ref[...], preferred_element_type=F32)
    gate_m = mod_ref[0, 2:3, :]
    shift_f = mod_ref[0, 3:4, :]
    scale_f = mod_ref[0, 4:5, :]
    x1 = x_ref[0] + gate_m * (y * _rms_scale(y) * gpost_ref[...])
    x1_ref[0] = x1
    h2 = x1 * _rms_scale(x1) * gpre_ref[...] * (1.0 + scale_f) + shift_f
    h2_ref[0] = h2.astype(BF16)


def _outproj_call(x, mod3, g_post, g_pre, w_out, pool, outs, lses):
    batch, seq, _ = x.shape
    tm = ROW_TILE
    row = lambda b, s: (b, s, 0)
    const = lambda b, s: (0, 0)
    grp_spec = pl.BlockSpec((1, tm, GROUP_WIDTH), row)
    return pl.pallas_call(
        _outproj_kernel,
        grid=(batch, seq // tm),
        in_specs=[pl.BlockSpec((1, tm, D_MODEL), row),
                  pl.BlockSpec((1, N_MOD, D_MODEL), lambda b, s: (b, 0, 0)),
                  pl.BlockSpec((1, D_MODEL), const),
                  pl.BlockSpec((1, D_MODEL), const),
                  pl.BlockSpec((POOL_WIDTH + GROUP_WIDTH, D_MODEL), const)] + [grp_spec] * 7,
        out_specs=[pl.BlockSpec((1, tm, D_MODEL), row)] * 2,
        out_shape=[jax.ShapeDtypeStruct((batch, seq, D_MODEL), F32),
                   jax.ShapeDtypeStruct((batch, seq, D_MODEL), BF16)],
        compiler_params=_params(2),
        name="outproj",
    )(x, mod3, g_post, g_pre, w_out, pool, *outs, *lses)


def _ffn_kernel(x1_ref, h2_ref, mod_ref, gpost_ref, wup_ref, cw_ref, cb_ref, wdown_ref,
                out_ref, act_ref, carry_ref):
    s = pl.program_id(1)
    tm = h2_ref.shape[1]
    halo = carry_ref.shape[0]

    @pl.when(s == 0)
    def _():
        carry_ref[...] = jnp.zeros_like(carry_ref)

    h2 = h2_ref[0]
    for c in range(D_FF // FF_CHUNK):
        cols = slice(c * FF_CHUNK, (c + 1) * FF_CHUNK)
        vcols = slice(D_FF + c * FF_CHUNK, D_FF + (c + 1) * FF_CHUNK)
        gate = jnp.dot(h2, wup_ref[:, cols], preferred_element_type=F32)
        val = jnp.dot(h2, wup_ref[:, vcols], preferred_element_type=F32)
        ext = jnp.concatenate([carry_ref[:, cols], gate], axis=0)
        carry_ref[:, cols] = gate[tm - halo:, :]
        back1 = pltpu.roll(ext, 1, 0)[halo:, :]
        back2 = pltpu.roll(ext, 2, 0)[halo:, :]
        conv = (back2 * cw_ref[0:1, cols] + back1 * cw_ref[1:2, cols] + gate * cw_ref[2:3, cols]
                + cb_ref[:, cols])
        act_ref[:, cols] = (jax.nn.gelu(conv, approximate=True) * val).astype(BF16)
    y = jnp.dot(act_ref[...], wdown_ref[...], preferred_element_type=F32)
    gate_f = mod_ref[0, 5:6, :]
    out_ref[0] = x1_ref[0] + gate_f * (y * _rms_scale(y) * gpost_ref[...])


def _ffn_call(x1, h2, mod3, g_post, w_up, conv_w, conv_b, w_down):
    batch, seq, _ = x1.shape
    tm = ROW_TILE
    row = lambda b, s: (b, s, 0)
    const = lambda b, s: (0, 0)
    resident = dict(pipeline_mode=pl.Buffered(1))
    return pl.pallas_call(
        _ffn_kernel,
        grid=(batch, seq // tm),
        in_specs=[pl.BlockSpec((1, tm, D_MODEL), row),
                  pl.BlockSpec((1, tm, D_MODEL), row),
                  pl.BlockSpec((1, N_MOD, D_MODEL), lambda b, s: (b, 0, 0)),
                  pl.BlockSpec((1, D_MODEL), const),
                  pl.BlockSpec((D_MODEL, 2 * D_FF), const, **resident),
                  pl.BlockSpec((3, D_FF), const),
                  pl.BlockSpec((1, D_FF), const),
                  pl.BlockSpec((D_FF, D_MODEL), const, **resident)],
        out_specs=pl.BlockSpec((1, tm, D_MODEL), row),
        out_shape=jax.ShapeDtypeStruct((batch, seq, D_MODEL), F32),
        scratch_shapes=[pltpu.VMEM((tm, D_FF), BF16),
                        pltpu.VMEM((8, D_FF), F32)],
        compiler_params=_params(2),
        name="ffn",
    )(x1, h2, mod3, g_post, w_up, conv_w, conv_b, w_down)


def kernel(x, c, positions, w_ada, b_ada, g_pre_mix, g_post_mix, g_pre_ffn, g_post_ffn,
           w_in, w_pool, b_pool, pool_scale, w_out, w_up, conv_w, conv_b, w_down):
    depth = w_ada.shape[0]
    batch = x.shape[0]
    cos_t, sin_t = _rope_tables(positions)
    for l in range(depth):
        mod3 = _mod_call(c, w_ada[l], b_ada[l]).reshape(batch, N_MOD, D_MODEL)
        wpool_blk = jax.scipy.linalg.block_diag(*[w_pool[l, g] for g in range(len(POOL_WINDOWS))])
        pool, q0, q1, q2, k0, k1, k2, v0, v1, v2 = _inproj_call(
            x, mod3, g_pre_mix[l].reshape(1, D_MODEL), w_in[l].astype(BF16), cos_t, sin_t,
            wpool_blk.astype(BF16), b_pool[l].reshape(1, POOL_WIDTH), pool_scale[l].reshape(1, POOL_WIDTH))
        outs, lses = [], []
        for (q, k, v), d in zip(((q0, k0, v0), (q1, k1, v1), (q2, k2, v2)), DILATIONS):
            o, lse = _attn_call(q, k, v, d)
            outs.append(o)
            lses.append(lse)
        x1, h2 = _outproj_call(x, mod3, g_post_mix[l].reshape(1, D_MODEL), g_pre_ffn[l].reshape(1, D_MODEL),
                               w_out[l].astype(BF16), pool, outs, lses)
        x = _ffn_call(x1, h2, mod3, g_post_ffn[l].reshape(1, D_MODEL), w_up[l].astype(BF16),
                      conv_w[l], conv_b[l].reshape(1, D_FF), w_down[l].astype(BF16))
    return x
```

```python
import functools

import jax
import jax.numpy as jnp
from jax import lax
from jax.experimental import pallas as pl
from jax.experimental.pallas import tpu as pltpu

F32 = jnp.float32
BF16 = jnp.bfloat16

D_MODEL = 1024
HEAD_DIM = 64
POOL_WIDTH = 256
POOL_WINDOWS = (2, 4, 8, 16)
POOL_GROUP_DIM = 64
MAX_POOL_WINDOW = 16
ATTN_WIDTH = 768
GROUP_WIDTH = 256
HEADS_PER_GROUP = 4
DILATIONS = (1, 4, 16)
SPAN = 128
BLOCK = 128
IN_PROJ_WIDTH = 2560
ROT_DIM = 16
ROT_HALF = 8
ROPE_THETA = 500000.0
D_FF = 2816
NORM_EPS = 1e-6
N_MOD = 6
MASK_VALUE = -1e30

LANES = 128
VMEM_LIMIT_BYTES = 56 * 1024 * 1024

ROW_TILE = 512
FF_CHUNK = 256
ATTN_CHUNK = BLOCK * DILATIONS[-1]
HALVES = GROUP_WIDTH // LANES


def _params(n_axes, vmem=VMEM_LIMIT_BYTES):
    return pltpu.CompilerParams(dimension_semantics=("arbitrary",) * n_axes, vmem_limit_bytes=vmem)


def _rms_scale(v):
    return lax.rsqrt(jnp.mean(v * v, axis=-1, keepdims=True) + NORM_EPS)


def _mod_kernel(c_ref, w_ref, b_ref, o_ref):
    c = c_ref[...]
    c_act = c * jax.nn.sigmoid(c)
    o_ref[...] = jnp.dot(c_act.astype(BF16), w_ref[...].astype(BF16),
                         preferred_element_type=F32) + b_ref[...]


def _mod_call(c, w_ada, b_ada):
    batch = c.shape[0]
    n = w_ada.shape[1]
    tn = 768
    return pl.pallas_call(
        _mod_kernel,
        grid=(n // tn,),
        in_specs=[pl.BlockSpec((batch, D_MODEL), lambda j: (0, 0)),
                  pl.BlockSpec((D_MODEL, tn), lambda j: (0, j)),
                  pl.BlockSpec((1, tn), lambda j: (0, j))],
        out_specs=pl.BlockSpec((batch, tn), lambda j: (0, j)),
        out_shape=jax.ShapeDtypeStruct((batch, n), F32),
        compiler_params=_params(1),
        name="mod",
    )(c, w_ada, b_ada.reshape(1, n))


def _inproj_kernel(x_ref, mod_ref, g_ref, w_ref, pos_ref, freq_ref, wpool_ref, bpool_ref, pscale_ref,
                   pool_ref, q0_ref, k0_ref, v0_ref, q1_ref, k1_ref, v1_ref, q2_ref, k2_ref, v2_ref,
                   carry_ref, stage_ref):
    s = pl.program_id(1)
    tm = x_ref.shape[1]
    x = x_ref[0]
    shift = mod_ref[0, 0:1, :]
    scale = mod_ref[0, 1:2, :]
    h = x * _rms_scale(x) * g_ref[...] * (1.0 + scale) + shift
    proj = jnp.dot(h.astype(BF16), w_ref[...], preferred_element_type=F32)

    n_pos_rows = tm // LANES
    pos = pos_ref[0, 0].astype(F32)
    pos_t = jnp.concatenate([pos, jnp.zeros((LANES - n_pos_rows, LANES), F32)], axis=0).T
    lane = lax.broadcasted_iota(jnp.int32, (LANES, LANES), 1) % HEAD_DIM
    cos_parts, sin_parts = [], []
    for j in range(n_pos_rows):
        ang = pos_t[:, j:j + 1] * freq_ref[...]
        sin = jnp.sin(ang)
        cos_parts.append(jnp.where(lane < ROT_DIM, jnp.cos(ang), 1.0))
        sin_parts.append(jnp.where(lane < ROT_HALF, -sin, jnp.where(lane < ROT_DIM, sin, 0.0)))
    cos = jnp.concatenate(cos_parts, axis=0)
    sin = jnp.concatenate(sin_parts, axis=0)
    first_half = (lax.broadcasted_iota(jnp.int32, (tm, LANES), 1) % HEAD_DIM) < ROT_HALF

    def rope(t):
        partner = jnp.where(first_half, pltpu.roll(t, LANES - ROT_HALF, 1), pltpu.roll(t, ROT_HALF, 1))
        return t * cos + partner * sin

    def emit(out_ref, value, d, half, stage):
        cols = slice(half * LANES, (half + 1) * LANES)
        if d == 1:
            out_ref[0, :, cols] = value.astype(BF16)
            return
        stage_ref[stage] = value
        for r in range(d):
            picked = stage_ref[stage, pl.ds(r, tm // d, stride=d), :]
            out_ref[0, :, r * GROUP_WIDTH + half * LANES:r * GROUP_WIDTH + (half + 1) * LANES] = (
                picked.astype(BF16))

    out_refs = ((q0_ref, k0_ref, v0_ref), (q1_ref, k1_ref, v1_ref), (q2_ref, k2_ref, v2_ref))
    q_scale = HEAD_DIM ** -0.5
    stage = 0
    for g, d in enumerate(DILATIONS):
        q_ref, k_ref, v_ref = out_refs[g]
        for half in range(HALVES):
            lo = POOL_WIDTH + g * GROUP_WIDTH + half * LANES
            q = proj[:, lo:lo + LANES]
            k = proj[:, lo + ATTN_WIDTH:lo + ATTN_WIDTH + LANES]
            v = proj[:, lo + 2 * ATTN_WIDTH:lo + 2 * ATTN_WIDTH + LANES]
            emit(q_ref, rope(q) * q_scale, d, half, stage)
            emit(k_ref, rope(k), d, half, stage + 1)
            emit(v_ref, v, d, half, stage + 2)
            if d > 1:
                stage += 3

    @pl.when(s == 0)
    def _():
        carry_ref[...] = jnp.zeros_like(carry_ref)

    u = proj[:, :POOL_WIDTH]
    ext = jnp.concatenate([carry_ref[...], u], axis=0)
    carry_ref[...] = u[tm - MAX_POOL_WINDOW:, :]
    sums = []
    acc = ext
    width = 1
    for w in POOL_WINDOWS:
        while width < w:
            acc = acc + pltpu.roll(acc, width, 0)
            width *= 2
        sums.append(acc[MAX_POOL_WINDOW:, :])
    lane_p = lax.broadcasted_iota(jnp.int32, (tm, POOL_WIDTH), 1)
    row_p = lax.broadcasted_iota(jnp.int32, (tm, POOL_WIDTH), 0) + s * tm
    group = lane_p // POOL_GROUP_DIM
    win = sums[-1]
    window = jnp.full((tm, POOL_WIDTH), POOL_WINDOWS[-1], jnp.int32)
    for gi in range(len(POOL_WINDOWS) - 2, -1, -1):
        win = jnp.where(group == gi, sums[gi], win)
        window = jnp.where(group == gi, POOL_WINDOWS[gi], window)
    count = jnp.minimum(row_p + 1, window).astype(F32)
    mixed = win / count - u
    y = jnp.dot(mixed.astype(BF16), wpool_ref[...], preferred_element_type=F32) + bpool_ref[...]
    pool_ref[0] = (y * pscale_ref[...]).astype(BF16)


def _inproj_call(x, mod3, g_pre, w_in, positions, wpool_blk, bpool, pscale):
    batch, seq, _ = x.shape
    tm = ROW_TILE
    row = lambda b, s: (b, s, 0)
    const = lambda b, s: (0, 0)
    inv_freq = ROPE_THETA ** (-jnp.arange(0, ROT_DIM, 2, dtype=F32) / ROT_DIM)
    freq = jnp.tile(inv_freq, LANES // ROT_HALF).reshape(1, LANES)
    pos4 = positions.reshape(batch, seq // tm, tm // LANES, LANES)
    out_specs = [pl.BlockSpec((1, tm, GROUP_WIDTH), row)]
    out_shape = [jax.ShapeDtypeStruct((batch, seq, GROUP_WIDTH), BF16)]
    n_stage = 0
    for d in DILATIONS:
        out_specs += [pl.BlockSpec((1, tm // d, d * GROUP_WIDTH), row)] * 3
        out_shape += [jax.ShapeDtypeStruct((batch, seq // d, d * GROUP_WIDTH), BF16)] * 3
        n_stage += 3 * HALVES if d > 1 else 0
    return pl.pallas_call(
        _inproj_kernel,
        grid=(batch, seq // tm),
        in_specs=[pl.BlockSpec((1, tm, D_MODEL), row),
                  pl.BlockSpec((1, N_MOD, D_MODEL), lambda b, s: (b, 0, 0)),
                  pl.BlockSpec((1, D_MODEL), const),
                  pl.BlockSpec((D_MODEL, IN_PROJ_WIDTH), const),
                  pl.BlockSpec((1, 1, tm // LANES, LANES), lambda b, s: (b, s, 0, 0)),
                  pl.BlockSpec((1, LANES), const),
                  pl.BlockSpec((POOL_WIDTH, POOL_WIDTH), const),
                  pl.BlockSpec((1, POOL_WIDTH), const),
                  pl.BlockSpec((1, POOL_WIDTH), const)],
        out_specs=out_specs,
        out_shape=out_shape,
        scratch_shapes=[pltpu.VMEM((MAX_POOL_WINDOW, POOL_WIDTH), F32),
                        pltpu.VMEM((n_stage, tm, LANES), F32)],
        compiler_params=_params(2),
        name="inproj",
    )(x, mod3, g_pre, w_in, pos4, freq, wpool_blk, bpool, pscale)


def _attend(q, keys, vals, bias, head_masks):
    q_heads = jnp.concatenate([jnp.where(m, q, jnp.zeros_like(q)) for m in head_masks], axis=0)
    scores = lax.dot_general(q_heads, keys, (((1,), (1,)), ((), ())), preferred_element_type=F32)
    scores = scores + bias
    m_row = jnp.max(scores, axis=-1, keepdims=True)
    p = jnp.exp(scores - m_row)
    den = jnp.sum(p, axis=-1, keepdims=True)
    pv = jnp.dot(p.astype(BF16), vals, preferred_element_type=F32) * (1.0 / den)
    lse = m_row + jnp.log(den)
    out = jnp.zeros((BLOCK, GROUP_WIDTH), F32)
    lse_out = jnp.zeros((BLOCK, GROUP_WIDTH), F32)
    for h, m in enumerate(head_masks):
        hrows = slice(h * BLOCK, (h + 1) * BLOCK)
        out = jnp.where(m, pv[hrows], out)
        lse_out = jnp.where(m, lse[hrows], lse_out)
    return out, lse_out


def _attn_kernel(q0_ref, kp0_ref, kc0_ref, vp0_ref, vc0_ref,
                 q1_ref, kp1_ref, kc1_ref, vp1_ref, vc1_ref,
                 q2_ref, kp2_ref, kc2_ref, vp2_ref, vc2_ref,
                 o_ref, ost_ref, lst_ref):
    first_chunk = pl.program_id(1) == 0
    i = lax.broadcasted_iota(jnp.int32, (HEADS_PER_GROUP * BLOCK, 2 * BLOCK), 0) % BLOCK
    j = lax.broadcasted_iota(jnp.int32, (HEADS_PER_GROUP * BLOCK, 2 * BLOCK), 1)
    valid = (j >= i) & (j <= i + SPAN)
    bias = jnp.where(valid, 0.0, MASK_VALUE)
    has_prev = jnp.logical_not(first_chunk)
    bias_first = jnp.where(valid & ((j >= BLOCK) | has_prev), 0.0, MASK_VALUE)
    lane = lax.broadcasted_iota(jnp.int32, (BLOCK, GROUP_WIDTH), 1)
    head_masks = [(lane // HEAD_DIM) == h for h in range(HEADS_PER_GROUP)]

    def tile(q_ref, kp_ref, kc_ref, vp_ref, vc_ref, n, r):
        cols = slice(r * GROUP_WIDTH, (r + 1) * GROUP_WIDTH)
        rows = slice(n * BLOCK, (n + 1) * BLOCK)
        if n == 0:
            keys = jnp.concatenate([kp_ref[0, :, cols], kc_ref[0, rows, cols]], axis=0)
            vals = jnp.concatenate([vp_ref[0, :, cols], vc_ref[0, rows, cols]], axis=0)
            tile_bias = bias_first
        else:
            both = slice((n - 1) * BLOCK, (n + 1) * BLOCK)
            keys, vals, tile_bias = kc_ref[0, both, cols], vc_ref[0, both, cols], bias
        return _attend(q_ref[0, rows, cols], keys, vals, tile_bias, head_masks)

    def merge(o_a, l_a, o_b, l_b):
        top = jnp.maximum(l_a, l_b)
        w_a = jnp.exp(l_a - top)
        w_b = jnp.exp(l_b - top)
        total = w_a + w_b
        return (w_a * o_a + w_b * o_b) * (1.0 / total), top + jnp.log(total)

    d2, d1 = DILATIONS[2], DILATIONS[1]
    for r in range(d2):
        out, lse = tile(q2_ref, kp2_ref, kc2_ref, vp2_ref, vc2_ref, 0, r)
        rows = pl.ds(r, BLOCK, stride=d2)
        for half in range(HALVES):
            cols = slice(half * LANES, (half + 1) * LANES)
            ost_ref[half, rows, :] = out[:, cols]
            lst_ref[half, rows, :] = lse[:, cols]
    for n in range(ATTN_CHUNK // (d1 * BLOCK)):
        for r in range(d1):
            out, lse = tile(q1_ref, kp1_ref, kc1_ref, vp1_ref, vc1_ref, n, r)
            rows = pl.ds(n * d1 * BLOCK + r, BLOCK, stride=d1)
            for half in range(HALVES):
                cols = slice(half * LANES, (half + 1) * LANES)
                o_m, l_m = merge(ost_ref[half, rows, :], lst_ref[half, rows, :], out[:, cols], lse[:, cols])
                ost_ref[half, rows, :] = o_m
                lst_ref[half, rows, :] = l_m
    for n in range(ATTN_CHUNK // BLOCK):
        out, lse = tile(q0_ref, kp0_ref, kc0_ref, vp0_ref, vc0_ref, n, 0)
        rows = slice(n * BLOCK, (n + 1) * BLOCK)
        for half in range(HALVES):
            cols = slice(half * LANES, (half + 1) * LANES)
            o_m, _ = merge(ost_ref[half, rows, :], lst_ref[half, rows, :], out[:, cols], lse[:, cols])
            o_ref[0, rows, cols] = o_m.astype(BF16)


def _attn_call(qkv):
    batch = qkv[0][0].shape[0]
    seq = qkv[0][0].shape[1]
    in_specs, args = [], []
    for (q, k, v), d in zip(qkv, DILATIONS):
        rows = ATTN_CHUNK // d
        width = d * GROUP_WIDTH
        cur = pl.BlockSpec((1, rows, width), lambda b, c: (b, c, 0))
        blocks_per_chunk = rows // BLOCK
        prev = pl.BlockSpec((1, BLOCK, width),
                            lambda b, c, n=blocks_per_chunk: (b, jnp.maximum(c * n - 1, 0), 0))
        in_specs += [cur, prev, cur, prev, cur]
        args += [q, k, k, v, v]
    return pl.pallas_call(
        _attn_kernel,
        grid=(batch, seq // ATTN_CHUNK),
        in_specs=in_specs,
        out_specs=pl.BlockSpec((1, ATTN_CHUNK, GROUP_WIDTH), lambda b, c: (b, c, 0)),
        out_shape=jax.ShapeDtypeStruct((batch, seq, GROUP_WIDTH), BF16),
        scratch_shapes=[pltpu.VMEM((HALVES, ATTN_CHUNK, LANES), F32),
                        pltpu.VMEM((HALVES, ATTN_CHUNK, LANES), F32)],
        compiler_params=_params(2),
        name="attn",
    )(*args)


def _outproj_kernel(x_ref, mod_ref, gpost_ref, gpre_ref, w_ref, pool_ref, attn_ref, x1_ref, h2_ref):
    mixed = jnp.concatenate([pool_ref[0], attn_ref[0]], axis=-1)
    y = jnp.dot(mixed, w_ref[...], preferred_element_type=F32)
    gate_m = mod_ref[0, 2:3, :]
    shift_f = mod_ref[0, 3:4, :]
    scale_f = mod_ref[0, 4:5, :]
    x1 = x_ref[0] + gate_m * (y * _rms_scale(y) * gpost_ref[...])
    x1_ref[0] = x1
    h2 = x1 * _rms_scale(x1) * gpre_ref[...] * (1.0 + scale_f) + shift_f
    h2_ref[0] = h2.astype(BF16)


def _outproj_call(x, mod3, g_post, g_pre, w_out, pool, attn):
    batch, seq, _ = x.shape
    tm = ROW_TILE
    row = lambda b, s: (b, s, 0)
    const = lambda b, s: (0, 0)
    grp_spec = pl.BlockSpec((1, tm, GROUP_WIDTH), row)
    return pl.pallas_call(
        _outproj_kernel,
        grid=(batch, seq // tm),
        in_specs=[pl.BlockSpec((1, tm, D_MODEL), row),
                  pl.BlockSpec((1, N_MOD, D_MODEL), lambda b, s: (b, 0, 0)),
                  pl.BlockSpec((1, D_MODEL), const),
                  pl.BlockSpec((1, D_MODEL), const),
                  pl.BlockSpec((POOL_WIDTH + GROUP_WIDTH, D_MODEL), const),
                  grp_spec, grp_spec],
        out_specs=[pl.BlockSpec((1, tm, D_MODEL), row)] * 2,
        out_shape=[jax.ShapeDtypeStruct((batch, seq, D_MODEL), F32),
                   jax.ShapeDtypeStruct((batch, seq, D_MODEL), BF16)],
        compiler_params=_params(2),
        name="outproj",
    )(x, mod3, g_post, g_pre, w_out, pool, attn)


def _ffn_kernel(x1_ref, h2_ref, mod_ref, gpost_ref, wup_ref, cw_ref, cb_ref, wdown_ref,
                out_ref, act_ref, carry_ref):
    s = pl.program_id(1)
    tm = h2_ref.shape[1]
    halo = carry_ref.shape[0]

    @pl.when(s == 0)
    def _():
        carry_ref[...] = jnp.zeros_like(carry_ref)

    h2 = h2_ref[0]
    for c in range(D_FF // FF_CHUNK):
        cols = slice(c * FF_CHUNK, (c + 1) * FF_CHUNK)
        vcols = slice(D_FF + c * FF_CHUNK, D_FF + (c + 1) * FF_CHUNK)
        gate = jnp.dot(h2, wup_ref[:, cols], preferred_element_type=F32)
        val = jnp.dot(h2, wup_ref[:, vcols], preferred_element_type=F32)
        ext = jnp.concatenate([carry_ref[:, cols], gate], axis=0)
        carry_ref[:, cols] = gate[tm - halo:, :]
        back1 = pltpu.roll(ext, 1, 0)[halo:, :]
        back2 = pltpu.roll(ext, 2, 0)[halo:, :]
        conv = (back2 * cw_ref[0:1, cols] + back1 * cw_ref[1:2, cols] + gate * cw_ref[2:3, cols]
                + cb_ref[:, cols])
        act_ref[:, cols] = (jax.nn.gelu(conv, approximate=True) * val).astype(BF16)
    y = jnp.dot(act_ref[...], wdown_ref[...], preferred_element_type=F32)
    gate_f = mod_ref[0, 5:6, :]
    out_ref[0] = x1_ref[0] + gate_f * (y * _rms_scale(y) * gpost_ref[...])


def _ffn_call(x1, h2, mod3, g_post, w_up, conv_w, conv_b, w_down):
    batch, seq, _ = x1.shape
    tm = ROW_TILE
    row = lambda b, s: (b, s, 0)
    const = lambda b, s: (0, 0)
    resident = dict(pipeline_mode=pl.Buffered(1))
    return pl.pallas_call(
        _ffn_kernel,
        grid=(batch, seq // tm),
        in_specs=[pl.BlockSpec((1, tm, D_MODEL), row),
                  pl.BlockSpec((1, tm, D_MODEL), row),
                  pl.BlockSpec((1, N_MOD, D_MODEL), lambda b, s: (b, 0, 0)),
                  pl.BlockSpec((1, D_MODEL), const),
                  pl.BlockSpec((D_MODEL, 2 * D_FF), const, **resident),
                  pl.BlockSpec((3, D_FF), const),
                  pl.BlockSpec((1, D_FF), const),
                  pl.BlockSpec((D_FF, D_MODEL), const, **resident)],
        out_specs=pl.BlockSpec((1, tm, D_MODEL), row),
        out_shape=jax.ShapeDtypeStruct((batch, seq, D_MODEL), F32),
        scratch_shapes=[pltpu.VMEM((tm, D_FF), BF16),
                        pltpu.VMEM((8, D_FF), F32)],
        compiler_params=_params(2),
        name="ffn",
    )(x1, h2, mod3, g_post, w_up, conv_w, conv_b, w_down)


def kernel(x, c, positions, w_ada, b_ada, g_pre_mix, g_post_mix, g_pre_ffn, g_post_ffn,
           w_in, w_pool, b_pool, pool_scale, w_out, w_up, conv_w, conv_b, w_down):
    depth = w_ada.shape[0]
    batch = x.shape[0]
    for l in range(depth):
        mod3 = _mod_call(c, w_ada[l], b_ada[l]).reshape(batch, N_MOD, D_MODEL)
        wpool_blk = jax.scipy.linalg.block_diag(*[w_pool[l, g] for g in range(len(POOL_WINDOWS))])
        pool, *qkv = _inproj_call(
            x, mod3, g_pre_mix[l].reshape(1, D_MODEL), w_in[l].astype(BF16), positions,
            wpool_blk.astype(BF16), b_pool[l].reshape(1, POOL_WIDTH), pool_scale[l].reshape(1, POOL_WIDTH))
        attn = _attn_call([qkv[0:3], qkv[3:6], qkv[6:9]])
        x1, h2 = _outproj_call(x, mod3, g_post_mix[l].reshape(1, D_MODEL), g_pre_ffn[l].reshape(1, D_MODEL),
                               w_out[l].astype(BF16), pool, attn)
        x = _ffn_call(x1, h2, mod3, g_post_ffn[l].reshape(1, D_MODEL), w_up[l].astype(BF16),
                      conv_w[l], conv_b[l].reshape(1, D_FF), w_down[l].astype(BF16))
    return x
```

```python
import numpy as np

import jax
import jax.numpy as jnp
from jax import lax
from jax.experimental import pallas as pl
from jax.experimental.pallas import tpu as pltpu

F32 = jnp.float32
BF16 = jnp.bfloat16

D_MODEL = 1024
HEAD_DIM = 64
POOL_WIDTH = 256
POOL_WINDOWS = (2, 4, 8, 16)
POOL_GROUP_DIM = 64
MAX_POOL_WINDOW = 16
ATTN_WIDTH = 768
GROUP_WIDTH = 256
HEADS_PER_GROUP = 4
DILATIONS = (1, 4, 16)
DESTRIDE_STEP = 4
SPAN = 128
BLOCK = 128
IN_PROJ_WIDTH = 2560
ROT_DIM = 16
ROT_HALF = 8
ROPE_THETA = 500000.0
D_FF = 2816
NORM_EPS = 1e-6
N_MOD = 6
MASK_VALUE = -1e30

LANES = 128
SUBLANES = 8
VMEM_LIMIT_BYTES = 56 * 1024 * 1024

ROW_TILE = 512
FFN_PIECE = 128
INPROJ_TILE = 1024
INPROJ_SUB = 512
PIECE = 64
FF_CHUNK = 256
ATTN_CHUNK = BLOCK * DILATIONS[-1]
HALVES = GROUP_WIDTH // LANES
N_SPLIT = 3


def _params(n_axes, vmem=VMEM_LIMIT_BYTES):
    return pltpu.CompilerParams(dimension_semantics=("arbitrary",) * n_axes, vmem_limit_bytes=vmem)


def _rms_scale(v):
    return lax.rsqrt(jnp.mean(v * v, axis=-1, keepdims=True) + NORM_EPS)


def _mod_kernel(c_ref, w_ref, b_ref, o_ref):
    c = c_ref[...]
    c_act = c * jax.nn.sigmoid(c)
    o_ref[...] = jnp.dot(c_act.astype(BF16), w_ref[...].astype(BF16),
                         preferred_element_type=F32) + b_ref[...]


def _mod_call(c, w_ada, b_ada):
    batch = c.shape[0]
    n = w_ada.shape[1]
    tn = 768
    return pl.pallas_call(
        _mod_kernel,
        grid=(n // tn,),
        in_specs=[pl.BlockSpec((batch, D_MODEL), lambda j: (0, 0)),
                  pl.BlockSpec((D_MODEL, tn), lambda j: (0, j)),
                  pl.BlockSpec((1, tn), lambda j: (0, j))],
        out_specs=pl.BlockSpec((batch, tn), lambda j: (0, j)),
        out_shape=jax.ShapeDtypeStruct((batch, n), F32),
        compiler_params=_params(1),
        name="mod",
    )(c, w_ada, b_ada.reshape(1, n))


def _rope_expansion():
    e = np.zeros((LANES, 2 * LANES), np.float32)
    ones_row = 2 * N_SPLIT * ROT_HALF
    for lane in range(LANES):
        in_head = lane % HEAD_DIM
        if in_head >= ROT_DIM:
            e[ones_row, lane] = 1.0
            continue
        sign = -1.0 if in_head < ROT_HALF else 1.0
        for t in range(N_SPLIT):
            e[t * ROT_HALF + in_head % ROT_HALF, lane] = 1.0
            e[(N_SPLIT + t) * ROT_HALF + in_head % ROT_HALF, LANES + lane] = sign
    return e


def _rope_tables(pos_rows, freq_col, expand):
    cos_parts, sin_parts = [], []
    for j in range(pos_rows.shape[0]):
        ang = freq_col * pos_rows[j:j + 1, :]
        terms = []
        for table in (jnp.cos(ang), jnp.sin(ang)):
            rest = table
            for _ in range(N_SPLIT):
                part = rest.astype(BF16).astype(F32)
                terms.append(part)
                rest = rest - part
        terms.append(jnp.ones((SUBLANES, LANES), F32))
        pad = jnp.zeros((LANES - SUBLANES * len(terms), LANES), F32)
        stacked = jnp.concatenate(terms + [pad], axis=0)
        both = jnp.dot(stacked.T.astype(BF16), expand, preferred_element_type=F32)
        cos_parts.append(both[:, :LANES])
        sin_parts.append(both[:, LANES:])
    return jnp.concatenate(cos_parts, axis=0), jnp.concatenate(sin_parts, axis=0)


def _inproj_kernel(x_ref, xn_ref, mod_ref, modn_ref, g_ref, w_ref, pos_ref, posn_ref, freq_ref, expand_ref,
                   wpool_ref, bpool_ref, pscale_ref,
                   pool_ref, q0_ref, k0_ref, v0_ref, q1_ref, k1_ref, v1_ref, q2_ref, k2_ref, v2_ref,
                   carry_ref, stage_ref, mid_ref, h_ref, tab_ref):
    s = pl.program_id(1)
    tm = x_ref.shape[1]
    sub = INPROJ_SUB
    q_scale = HEAD_DIM ** -0.5
    first_half = (lax.broadcasted_iota(jnp.int32, (sub, LANES), 1) % HEAD_DIM) < ROT_HALF
    low_group = lax.broadcasted_iota(jnp.int32, (sub + MAX_POOL_WINDOW, LANES), 1) < POOL_GROUP_DIM
    low_group_head = lax.broadcasted_iota(jnp.int32, (MAX_POOL_WINDOW, LANES), 1) < POOL_GROUP_DIM
    low_group_row = lax.broadcasted_iota(jnp.int32, (1, LANES), 1) < POOL_GROUP_DIM
    out_refs = ((q0_ref, k0_ref, v0_ref), (q1_ref, k1_ref, v1_ref), (q2_ref, k2_ref, v2_ref))

    @pl.when(s == 0)
    def _():
        carry_ref[...] = jnp.zeros_like(carry_ref)

    def rope(t, cos, sin):
        partner = jnp.where(first_half, pltpu.roll(t, LANES - ROT_HALF, 1), pltpu.roll(t, ROT_HALF, 1))
        return t * cos + partner * sin

    def emit(out_ref, value, d, half, a, slot):
        cols = slice(half * LANES, (half + 1) * LANES)
        rows_out = sub // d
        out_rows = slice(a * rows_out, (a + 1) * rows_out)
        if d == 1:
            out_ref[0, 0, out_rows, cols] = value.astype(BF16)
            return
        stage_ref[slot] = value
        step = DESTRIDE_STEP
        if d == step:
            for r in range(d):
                picked = stage_ref[slot, pl.ds(r, rows_out, stride=step), :]
                out_ref[0, r, out_rows, cols] = picked.astype(BF16)
            return
        part = sub // step
        for r_lo in range(step):
            mid_ref[slot, r_lo * part:(r_lo + 1) * part, :] = stage_ref[slot, pl.ds(r_lo, part, stride=step), :]
        for r_lo in range(step):
            for r_hi in range(step):
                r = r_lo + step * r_hi
                picked = mid_ref[slot, pl.ds(r_lo * part + r_hi, rows_out, stride=step), :]
                out_ref[0, r, out_rows, cols] = picked.astype(BF16)

    def pool_mix(u, row0):
        ext = jnp.concatenate([carry_ref[...], u], axis=0)
        carry_ref[...] = u[sub - MAX_POOL_WINDOW:, :]
        head = MAX_POOL_WINDOW
        head_row = lax.broadcasted_iota(jnp.int32, (head, LANES), 0) + row0 + 1
        mixed = []
        for half in range(POOL_WIDTH // LANES):
            w_a, w_b = POOL_WINDOWS[2 * half], POOL_WINDOWS[2 * half + 1]
            acc = ext[:, half * LANES:(half + 1) * LANES]
            width = 1
            while width < w_a:
                acc = acc + pltpu.roll(acc, width, 0)
                width *= 2
            sum_a = acc
            while width < w_b:
                acc = acc + pltpu.roll(acc, width, 0)
                width *= 2
            win = jnp.where(low_group, sum_a, acc)[MAX_POOL_WINDOW:, :]
            window = jnp.where(low_group_head, w_a, w_b)
            count = jnp.minimum(head_row, window).astype(F32)
            inv_window = jnp.where(low_group_row, 1.0 / w_a, 1.0 / w_b)
            mean = jnp.concatenate([win[:head] / count, win[head:] * inv_window], axis=0)
            mixed.append(mean - u[:, half * LANES:(half + 1) * LANES])
        mixed = jnp.concatenate(mixed, axis=1).astype(BF16)
        y = jnp.dot(mixed, wpool_ref[...], preferred_element_type=F32) + bpool_ref[...]
        return (y * pscale_ref[...]).astype(BF16)

    def prologue_piece(k, dst, xsrc_ref, row0, psrc_ref, prow0, gain, shift):
        rows = slice(k * PIECE, (k + 1) * PIECE)
        x = xsrc_ref[0, row0 + k * PIECE:row0 + (k + 1) * PIECE, :]
        h_ref[dst, rows, :] = (x * _rms_scale(x) * gain + shift).astype(BF16)
        if k < sub // LANES:
            pos = psrc_ref[0, 0, prow0 + k:prow0 + k + 1, :].astype(F32)
            cos, sin = _rope_tables(pos, freq_ref[...], expand_ref[...])
            for i, table in enumerate((cos, sin, cos * q_scale, sin * q_scale)):
                tab_ref[dst, i, k * LANES:(k + 1) * LANES, :] = table

    n_sub = tm // sub
    n_pieces = sub // PIECE
    shift = mod_ref[0, 0:1, :]
    gain = g_ref[...] * (1.0 + mod_ref[0, 1:2, :])
    shift_next = modn_ref[0, 0:1, :]
    gain_next = g_ref[...] * (1.0 + modn_ref[0, 1:2, :])

    @pl.when((pl.program_id(0) == 0) & (s == 0))
    def _():
        for k in range(n_pieces):
            prologue_piece(k, 0, x_ref, 0, pos_ref, 0, gain, shift)

    chunks = [(g, kind) for g in range(len(DILATIONS)) for kind in range(3)]
    chunks.remove((0, 2))
    chunks.insert(len(chunks) // 2, (None, None))
    chunks.append((0, 2))
    slots = {}
    for g, d in enumerate(DILATIONS):
        if d > 1:
            for kind in range(3):
                for half in range(HALVES):
                    slots[g, kind, half] = len(slots)

    for a in range(n_sub):
        rows = slice(a * sub, (a + 1) * sub)
        h = h_ref[a % 2]
        cos, sin, cos_q, sin_q = (tab_ref[a % 2, i] for i in range(4))
        for ci, (g, kind) in enumerate(chunks):
            if g is None:
                u = jnp.dot(h, w_ref[:, :POOL_WIDTH], preferred_element_type=F32)
                pool_ref[0, rows, :] = pool_mix(u, s * tm + a * sub)
            else:
                lo = POOL_WIDTH + kind * ATTN_WIDTH + g * GROUP_WIDTH
                proj = jnp.dot(h, w_ref[:, lo:lo + GROUP_WIDTH], preferred_element_type=F32)
                for half in range(HALVES):
                    value = proj[:, half * LANES:(half + 1) * LANES]
                    if kind == 0:
                        value = rope(value, cos_q, sin_q)
                    elif kind == 1:
                        value = rope(value, cos, sin)
                    emit(out_refs[g][kind], value, DILATIONS[g], half, a, slots.get((g, kind, half)))
            if ci < n_pieces:
                if a + 1 < n_sub:
                    prologue_piece(ci, (a + 1) % 2, x_ref, (a + 1) * sub, pos_ref, (a + 1) * (sub // LANES),
                                   gain, shift)
                else:
                    prologue_piece(ci, (a + 1) % 2, xn_ref, 0, posn_ref, 0, gain_next, shift_next)


def _inproj_call(x, mod3, g_pre, w_in, positions, wpool_blk, bpool, pscale):
    batch, seq, _ = x.shape
    tm = INPROJ_TILE
    row = lambda b, s: (b, s, 0)
    const = lambda b, s: (0, 0)
    resident = dict(pipeline_mode=pl.Buffered(1))
    inv_freq = ROPE_THETA ** (-jnp.arange(0, ROT_DIM, 2, dtype=F32) / ROT_DIM)
    pos4 = positions.reshape(batch, seq // tm, tm // LANES, LANES)
    expand = jnp.asarray(_rope_expansion(), BF16)
    steps = seq // tm

    def next_step(b, s):
        wrap = (s + 1) // steps
        return jnp.minimum(b + wrap, batch - 1), (s + 1) % steps

    out_specs = [pl.BlockSpec((1, tm, GROUP_WIDTH), row)]
    out_shape = [jax.ShapeDtypeStruct((batch, seq, GROUP_WIDTH), BF16)]
    n_stage = 0
    for d in DILATIONS:
        out_specs += [pl.BlockSpec((1, d, tm // d, GROUP_WIDTH), lambda b, s: (b, 0, s, 0))] * 3
        out_shape += [jax.ShapeDtypeStruct((batch, d, seq // d, GROUP_WIDTH), BF16)] * 3
        n_stage += 3 * HALVES if d > 1 else 0
    return pl.pallas_call(
        _inproj_kernel,
        grid=(batch, seq // tm),
        in_specs=[pl.BlockSpec((1, tm, D_MODEL), row),
                  pl.BlockSpec((1, INPROJ_SUB, D_MODEL),
                               lambda b, s: (next_step(b, s)[0], next_step(b, s)[1] * (tm // INPROJ_SUB), 0)),
                  pl.BlockSpec((1, N_MOD, D_MODEL), lambda b, s: (b, 0, 0)),
                  pl.BlockSpec((1, N_MOD, D_MODEL), lambda b, s: (next_step(b, s)[0], 0, 0)),
                  pl.BlockSpec((1, D_MODEL), const),
                  pl.BlockSpec((D_MODEL, IN_PROJ_WIDTH), const, **resident),
                  pl.BlockSpec((1, 1, tm // LANES, LANES), lambda b, s: (b, s, 0, 0)),
                  pl.BlockSpec((1, 1, tm // LANES, LANES), lambda b, s: (*next_step(b, s), 0, 0)),
                  pl.BlockSpec((ROT_HALF, 1), const),
                  pl.BlockSpec((LANES, 2 * LANES), const),
                  pl.BlockSpec((POOL_WIDTH, POOL_WIDTH), const),
                  pl.BlockSpec((1, POOL_WIDTH), const),
                  pl.BlockSpec((1, POOL_WIDTH), const)],
        out_specs=out_specs,
        out_shape=out_shape,
        scratch_shapes=[pltpu.VMEM((MAX_POOL_WINDOW, POOL_WIDTH), F32),
                        pltpu.VMEM((n_stage, INPROJ_SUB, LANES), F32),
                        pltpu.VMEM((n_stage, INPROJ_SUB, LANES), F32),
                        pltpu.VMEM((2, INPROJ_SUB, D_MODEL), BF16),
                        pltpu.VMEM((2, 4, INPROJ_SUB, LANES), F32)],
        compiler_params=_params(2),
        name="inproj",
    )(x, x, mod3, mod3, g_pre, w_in, pos4, pos4, inv_freq.reshape(ROT_HALF, 1), expand, wpool_blk, bpool,
      pscale)


def _attend(q, keys, vals, bias, head_masks):
    q_heads = jnp.concatenate([jnp.where(m, q, jnp.zeros_like(q)) for m in head_masks], axis=0)
    scores = lax.dot_general(q_heads, keys, (((1,), (1,)), ((), ())), preferred_element_type=F32)
    probs, lses = [], []
    for h in range(HEADS_PER_GROUP):
        s = scores[h * BLOCK:(h + 1) * BLOCK] + bias
        m_row = jnp.max(s, axis=-1, keepdims=True)
        p = jnp.exp(s - m_row)
        den = jnp.sum(p, axis=-1, keepdims=True)
        probs.append((p * (1.0 / den)).astype(BF16))
        lses.append(m_row + jnp.log(den))
    pv = jnp.dot(jnp.concatenate(probs, axis=0), vals, preferred_element_type=F32)
    out = pv[(HEADS_PER_GROUP - 1) * BLOCK:]
    lse_out = jnp.broadcast_to(lses[-1], (BLOCK, GROUP_WIDTH))
    for h in range(HEADS_PER_GROUP - 2, -1, -1):
        out = jnp.where(head_masks[h], pv[h * BLOCK:(h + 1) * BLOCK], out)
        lse_out = jnp.where(head_masks[h], lses[h], lse_out)
    return out, lse_out


def _attn_kernel(q0_ref, kp0_ref, kc0_ref, vp0_ref, vc0_ref,
                 q1_ref, kp1_ref, kc1_ref, vp1_ref, vc1_ref,
                 q2_ref, kp2_ref, kc2_ref, vp2_ref, vc2_ref,
                 o_ref, ost_ref, lst_ref):
    first_chunk = pl.program_id(1) == 0
    i = lax.broadcasted_iota(jnp.int32, (BLOCK, 2 * BLOCK), 0)
    j = lax.broadcasted_iota(jnp.int32, (BLOCK, 2 * BLOCK), 1)
    valid = (j >= i) & (j <= i + SPAN)
    bias = jnp.where(valid, 0.0, MASK_VALUE)
    has_prev = jnp.logical_not(first_chunk)
    bias_first = jnp.where(valid & ((j >= BLOCK) | has_prev), 0.0, MASK_VALUE)
    lane = lax.broadcasted_iota(jnp.int32, (BLOCK, GROUP_WIDTH), 1)
    head_masks = [(lane // HEAD_DIM) == h for h in range(HEADS_PER_GROUP)]
    groups = ((q0_ref, kp0_ref, kc0_ref, vp0_ref, vc0_ref),
              (q1_ref, kp1_ref, kc1_ref, vp1_ref, vc1_ref),
              (q2_ref, kp2_ref, kc2_ref, vp2_ref, vc2_ref))

    def first_tile(g, r):
        q_ref, kp_ref, kc_ref, vp_ref, vc_ref = groups[g]
        keys = jnp.concatenate([kp_ref[0, r], kc_ref[0, r, :BLOCK, :]], axis=0)
        vals = jnp.concatenate([vp_ref[0, r], vc_ref[0, r, :BLOCK, :]], axis=0)
        return _attend(q_ref[0, r, :BLOCK, :], keys, vals, bias_first, head_masks)

    def later_tile(g, r, n):
        q_ref, _, kc_ref, _, vc_ref = groups[g]
        start = pl.multiple_of((n - 1) * BLOCK, BLOCK)
        both = pl.ds(start, 2 * BLOCK)
        return _attend(q_ref[0, r, pl.ds(start + BLOCK, BLOCK), :], kc_ref[0, r, both, :],
                       vc_ref[0, r, both, :], bias, head_masks)

    def token_rows(g, r, n):
        d = DILATIONS[g]
        start = n * (d * BLOCK) + r
        if d > 1:
            return pl.ds(start, BLOCK, stride=d)
        return pl.ds(start if isinstance(start, int) else pl.multiple_of(start, BLOCK), BLOCK)

    def merged(rows, out, lse, half):
        cols = slice(half * LANES, (half + 1) * LANES)
        o_a, l_a, o_b, l_b = ost_ref[half, rows, :], lst_ref[half, rows, :], out[:, cols], lse[:, cols]
        top = jnp.maximum(l_a, l_b)
        w_a = jnp.exp(l_a - top)
        w_b = jnp.exp(l_b - top)
        total = w_a + w_b
        return (w_a * o_a + w_b * o_b) * (1.0 / total), top + jnp.log(total)

    def start_state(rows, out, lse):
        for half in range(HALVES):
            cols = slice(half * LANES, (half + 1) * LANES)
            ost_ref[half, rows, :] = out[:, cols]
            lst_ref[half, rows, :] = lse[:, cols]

    def fold_state(rows, out, lse):
        for half in range(HALVES):
            ost_ref[half, rows, :], lst_ref[half, rows, :] = merged(rows, out, lse, half)

    def finish(rows, out, lse):
        for half in range(HALVES):
            o_ref[0, rows, half * LANES:(half + 1) * LANES] = merged(rows, out, lse, half)[0].astype(BF16)

    def loop(lo, hi, unroll, body):
        def step(t, carry):
            body(t)
            return carry
        lax.fori_loop(lo, hi, step, 0, unroll=unroll)

    blocks = [ATTN_CHUNK // (d * BLOCK) for d in DILATIONS]
    loop(0, DILATIONS[2], 2, lambda r: start_state(token_rows(2, r, 0), *first_tile(2, r)))
    loop(0, DILATIONS[1], 2, lambda r: fold_state(token_rows(1, r, 0), *first_tile(1, r)))
    later = blocks[1] - 1

    def group1_later(t):
        r, n = t // later, 1 + t % later
        fold_state(token_rows(1, r, n), *later_tile(1, r, n))

    loop(0, DILATIONS[1] * later, 2, group1_later)
    finish(token_rows(0, 0, 0), *first_tile(0, 0))
    loop(1, blocks[0], 3, lambda n: finish(token_rows(0, 0, n), *later_tile(0, 0, n)))


def _attn_call(qkv):
    batch = qkv[0][0].shape[0]
    seq = qkv[0][0].shape[2]
    in_specs, args = [], []
    for (q, k, v), d in zip(qkv, DILATIONS):
        rows = ATTN_CHUNK // d
        cur = pl.BlockSpec((1, d, rows, GROUP_WIDTH), lambda b, c: (b, 0, c, 0))
        blocks_per_chunk = rows // BLOCK
        prev = pl.BlockSpec((1, d, BLOCK, GROUP_WIDTH),
                            lambda b, c, n=blocks_per_chunk: (b, 0, jnp.maximum(c * n - 1, 0), 0))
        in_specs += [cur, prev, cur, prev, cur]
        args += [q, k, k, v, v]
    return pl.pallas_call(
        _attn_kernel,
        grid=(batch, seq // ATTN_CHUNK),
        in_specs=in_specs,
        out_specs=pl.BlockSpec((1, ATTN_CHUNK, GROUP_WIDTH), lambda b, c: (b, c, 0)),
        out_shape=jax.ShapeDtypeStruct((batch, seq, GROUP_WIDTH), BF16),
        scratch_shapes=[pltpu.VMEM((HALVES, ATTN_CHUNK, LANES), F32),
                        pltpu.VMEM((HALVES, ATTN_CHUNK, LANES), F32)],
        compiler_params=_params(2),
        name="attn",
    )(*args)


def _ffn_kernel(x_ref, pool_ref, attn_ref, xn_ref, pooln_ref, attnn_ref, mod_ref, modn_ref,
                gpostm_ref, gpref_ref, gpostf_ref, wout_ref, wup_ref, cw_ref, cb_ref, wdown_ref,
                out_ref, act_ref, carry_ref, x1a_ref, x1b_ref, h2a_ref, h2b_ref):
    s = pl.program_id(1)
    tm = out_ref.shape[1]
    halo = carry_ref.shape[0]
    n_pieces = tm // FFN_PIECE

    @pl.when(s == 0)
    def _():
        carry_ref[...] = jnp.zeros_like(carry_ref)

    def prologue_piece(k, x1_dst, h2_dst, xs_ref, ps_ref, as_ref, ms_ref):
        rows = slice(k * FFN_PIECE, (k + 1) * FFN_PIECE)
        mixed = jnp.concatenate([ps_ref[0, rows, :], as_ref[0, rows, :]], axis=-1)
        y = jnp.dot(mixed, wout_ref[...], preferred_element_type=F32)
        x1 = xs_ref[0, rows, :] + ms_ref[0, 2:3, :] * (y * _rms_scale(y) * gpostm_ref[...])
        x1_dst[rows, :] = x1
        h2 = x1 * _rms_scale(x1) * gpref_ref[...] * (1.0 + ms_ref[0, 4:5, :]) + ms_ref[0, 3:4, :]
        h2_dst[rows, :] = h2.astype(BF16)

    def step(x1_cur, h2_cur, x1_next, h2_next):
        h2 = h2_cur[...]
        n_chunks = D_FF // FF_CHUNK
        for c in range(n_chunks):
            cols = slice(c * FF_CHUNK, (c + 1) * FF_CHUNK)
            vcols = slice(D_FF + c * FF_CHUNK, D_FF + (c + 1) * FF_CHUNK)
            gate = jnp.dot(h2, wup_ref[:, cols], preferred_element_type=F32)
            val = jnp.dot(h2, wup_ref[:, vcols], preferred_element_type=F32)
            ext = jnp.concatenate([carry_ref[:, cols], gate], axis=0)
            carry_ref[:, cols] = gate[tm - halo:, :]
            back1 = pltpu.roll(ext, 1, 0)[halo:, :]
            back2 = pltpu.roll(ext, 2, 0)[halo:, :]
            conv = (back2 * cw_ref[0:1, cols] + back1 * cw_ref[1:2, cols] + gate * cw_ref[2:3, cols]
                    + cb_ref[:, cols])
            act_ref[:, cols] = (jax.nn.gelu(conv, approximate=True) * val).astype(BF16)
            if c % 2 == 0 and c // 2 < n_pieces:
                prologue_piece(c // 2, x1_next, h2_next, xn_ref, pooln_ref, attnn_ref, modn_ref)
        y = jnp.dot(act_ref[...], wdown_ref[...], preferred_element_type=F32)
        out_ref[0] = x1_cur[...] + mod_ref[0, 5:6, :] * (y * _rms_scale(y) * gpostf_ref[...])

    @pl.when((pl.program_id(0) == 0) & (s == 0))
    def _():
        for k in range(n_pieces):
            prologue_piece(k, x1a_ref, h2a_ref, x_ref, pool_ref, attn_ref, mod_ref)

    @pl.when(s % 2 == 0)
    def _():
        step(x1a_ref, h2a_ref, x1b_ref, h2b_ref)

    @pl.when(s % 2 == 1)
    def _():
        step(x1b_ref, h2b_ref, x1a_ref, h2a_ref)


def _ffn_call(x, pool, attn, mod3, g_post_mix, g_pre_ffn, g_post_ffn, w_out, w_up, conv_w, conv_b, w_down):
    batch, seq, _ = x.shape
    tm = ROW_TILE
    steps = seq // tm
    assert steps % 2 == 0
    row = lambda b, s: (b, s, 0)
    const = lambda b, s: (0, 0)

    def nxt(b, s):
        return jnp.minimum(b + (s + 1) // steps, batch - 1), (s + 1) % steps, 0

    resident = dict(pipeline_mode=pl.Buffered(1))
    vec = pl.BlockSpec((1, D_MODEL), const)
    return pl.pallas_call(
        _ffn_kernel,
        grid=(batch, steps),
        in_specs=[pl.BlockSpec((1, tm, D_MODEL), row),
                  pl.BlockSpec((1, tm, GROUP_WIDTH), row),
                  pl.BlockSpec((1, tm, GROUP_WIDTH), row),
                  pl.BlockSpec((1, tm, D_MODEL), nxt),
                  pl.BlockSpec((1, tm, GROUP_WIDTH), nxt),
                  pl.BlockSpec((1, tm, GROUP_WIDTH), nxt),
                  pl.BlockSpec((1, N_MOD, D_MODEL), lambda b, s: (b, 0, 0)),
                  pl.BlockSpec((1, N_MOD, D_MODEL), lambda b, s: (nxt(b, s)[0], 0, 0)),
                  vec, vec, vec,
                  pl.BlockSpec((POOL_WIDTH + GROUP_WIDTH, D_MODEL), const, **resident),
                  pl.BlockSpec((D_MODEL, 2 * D_FF), const, **resident),
                  pl.BlockSpec((3, D_FF), const),
                  pl.BlockSpec((1, D_FF), const),
                  pl.BlockSpec((D_FF, D_MODEL), const, **resident)],
        out_specs=pl.BlockSpec((1, tm, D_MODEL), row),
        out_shape=jax.ShapeDtypeStruct((batch, seq, D_MODEL), F32),
        scratch_shapes=[pltpu.VMEM((tm, D_FF), BF16),
                        pltpu.VMEM((SUBLANES, D_FF), F32),
                        pltpu.VMEM((tm, D_MODEL), F32),
                        pltpu.VMEM((tm, D_MODEL), F32),
                        pltpu.VMEM((tm, D_MODEL), BF16),
                        pltpu.VMEM((tm, D_MODEL), BF16)],
        compiler_params=_params(2),
        name="ffn",
    )(x, pool, attn, x, pool, attn, mod3, mod3, g_post_mix, g_pre_ffn, g_post_ffn, w_out, w_up, conv_w,
      conv_b, w_down)


def kernel(x, c, positions, w_ada, b_ada, g_pre_mix, g_post_mix, g_pre_ffn, g_post_ffn,
           w_in, w_pool, b_pool, pool_scale, w_out, w_up, conv_w, conv_b, w_down):
    depth = w_ada.shape[0]
    batch = x.shape[0]
    for l in range(depth):
        mod3 = _mod_call(c, w_ada[l], b_ada[l]).reshape(batch, N_MOD, D_MODEL)
        wpool_blk = jax.scipy.linalg.block_diag(*[w_pool[l, g] for g in range(len(POOL_WINDOWS))])
        pool, *qkv = _inproj_call(
            x, mod3, g_pre_mix[l].reshape(1, D_MODEL), w_in[l].astype(BF16), positions,
            wpool_blk.astype(BF16), b_pool[l].reshape(1, POOL_WIDTH), pool_scale[l].reshape(1, POOL_WIDTH))
        attn = _attn_call([qkv[0:3], qkv[3:6], qkv[6:9]])
        x = _ffn_call(x, pool, attn, mod3, g_post_mix[l].reshape(1, D_MODEL), g_pre_ffn[l].reshape(1, D_MODEL),
                      g_post_ffn[l].reshape(1, D_MODEL), w_out[l].astype(BF16), w_up[l].astype(BF16),
                      conv_w[l], conv_b[l].reshape(1, D_FF), w_down[l].astype(BF16))
    return x
```

```python
import numpy as np

import jax
import jax.numpy as jnp
from jax import lax
from jax.experimental import pallas as pl
from jax.experimental.pallas import tpu as pltpu

F32 = jnp.float32
BF16 = jnp.bfloat16

D_MODEL = 1024
HEAD_DIM = 64
POOL_WIDTH = 256
POOL_WINDOWS = (2, 4, 8, 16)
POOL_GROUP_DIM = 64
MAX_POOL_WINDOW = 16
ATTN_WIDTH = 768
GROUP_WIDTH = 256
HEADS_PER_GROUP = 4
DILATIONS = (1, 4, 16)
DESTRIDE_STEP = 4
SPAN = 128
BLOCK = 128
IN_PROJ_WIDTH = 2560
ROT_DIM = 16
ROT_HALF = 8
ROPE_THETA = 500000.0
D_FF = 2816
NORM_EPS = 1e-6
N_MOD = 6
MASK_VALUE = -1e30

LANES = 128
SUBLANES = 8
VMEM_LIMIT_BYTES = 56 * 1024 * 1024

ROW_TILE = 512
FFN_PIECE = 128
DOWN_CHUNK = 256
INPROJ_TILE = 1024
INPROJ_SUB = 512
PIECE = 64
FF_CHUNK = 256
ATTN_CHUNK = BLOCK * DILATIONS[-1]
HALVES = GROUP_WIDTH // LANES
N_SPLIT = 3


def _params(n_axes, vmem=VMEM_LIMIT_BYTES):
    return pltpu.CompilerParams(dimension_semantics=("arbitrary",) * n_axes, vmem_limit_bytes=vmem)


def _rms_scale(v):
    return lax.rsqrt(jnp.mean(v * v, axis=-1, keepdims=True) + NORM_EPS)


def _mod_kernel(c_ref, w_ref, b_ref, o_ref):
    c = c_ref[...]
    c_act = c * jax.nn.sigmoid(c)
    o_ref[...] = jnp.dot(c_act.astype(BF16), w_ref[...].astype(BF16),
                         preferred_element_type=F32) + b_ref[...]


def _mod_call(c, w_ada, b_ada):
    batch = c.shape[0]
    n = w_ada.shape[1]
    tn = 768
    return pl.pallas_call(
        _mod_kernel,
        grid=(n // tn,),
        in_specs=[pl.BlockSpec((batch, D_MODEL), lambda j: (0, 0)),
                  pl.BlockSpec((D_MODEL, tn), lambda j: (0, j)),
                  pl.BlockSpec((1, tn), lambda j: (0, j))],
        out_specs=pl.BlockSpec((batch, tn), lambda j: (0, j)),
        out_shape=jax.ShapeDtypeStruct((batch, n), F32),
        compiler_params=_params(1),
        name="mod",
    )(c, w_ada, b_ada.reshape(1, n))


def _rope_expansion():
    e = np.zeros((LANES, 2 * LANES), np.float32)
    ones_row = 2 * N_SPLIT * ROT_HALF
    for lane in range(LANES):
        in_head = lane % HEAD_DIM
        if in_head >= ROT_DIM:
            e[ones_row, lane] = 1.0
            continue
        sign = -1.0 if in_head < ROT_HALF else 1.0
        for t in range(N_SPLIT):
            e[t * ROT_HALF + in_head % ROT_HALF, lane] = 1.0
            e[(N_SPLIT + t) * ROT_HALF + in_head % ROT_HALF, LANES + lane] = sign
    return e


def _rope_tables(pos_rows, freq_col, expand):
    cos_parts, sin_parts = [], []
    for j in range(pos_rows.shape[0]):
        ang = freq_col * pos_rows[j:j + 1, :]
        terms = []
        for table in (jnp.cos(ang), jnp.sin(ang)):
            rest = table
            for _ in range(N_SPLIT):
                part = rest.astype(BF16).astype(F32)
                terms.append(part)
                rest = rest - part
        terms.append(jnp.ones((SUBLANES, LANES), F32))
        pad = jnp.zeros((LANES - SUBLANES * len(terms), LANES), F32)
        stacked = jnp.concatenate(terms + [pad], axis=0)
        both = jnp.dot(stacked.T.astype(BF16), expand, preferred_element_type=F32)
        cos_parts.append(both[:, :LANES])
        sin_parts.append(both[:, LANES:])
    return jnp.concatenate(cos_parts, axis=0), jnp.concatenate(sin_parts, axis=0)


def _inproj_kernel(x_ref, xn_ref, mod_ref, modn_ref, g_ref, w_ref, pos_ref, posn_ref, freq_ref, expand_ref,
                   wpool_ref, bpool_ref, pscale_ref,
                   pool_ref, q0_ref, k0_ref, v0_ref, q1_ref, k1_ref, v1_ref, q2_ref, k2_ref, v2_ref,
                   carry_ref, stage_ref, mid_ref, h_ref, tab_ref, raw_ref):
    s = pl.program_id(1)
    tm = x_ref.shape[1]
    sub = INPROJ_SUB
    q_scale = HEAD_DIM ** -0.5
    first_half = (lax.broadcasted_iota(jnp.int32, (sub, LANES), 1) % HEAD_DIM) < ROT_HALF
    low_group = lax.broadcasted_iota(jnp.int32, (sub + MAX_POOL_WINDOW, LANES), 1) < POOL_GROUP_DIM
    low_group_head = lax.broadcasted_iota(jnp.int32, (MAX_POOL_WINDOW, LANES), 1) < POOL_GROUP_DIM
    low_group_row = lax.broadcasted_iota(jnp.int32, (1, LANES), 1) < POOL_GROUP_DIM
    out_refs = ((q0_ref, k0_ref, v0_ref), (q1_ref, k1_ref, v1_ref), (q2_ref, k2_ref, v2_ref))

    @pl.when(s == 0)
    def _():
        carry_ref[...] = jnp.zeros_like(carry_ref)

    def rope(t, cos, sin):
        partner = jnp.where(first_half, pltpu.roll(t, LANES - ROT_HALF, 1), pltpu.roll(t, ROT_HALF, 1))
        return t * cos + partner * sin

    def emit(out_ref, value, d, half, a, slot):
        cols = slice(half * LANES, (half + 1) * LANES)
        rows_out = sub // d
        out_rows = slice(a * rows_out, (a + 1) * rows_out)
        if d == 1:
            out_ref[0, 0, out_rows, cols] = value.astype(BF16)
            return
        stage_ref[slot] = value
        step = DESTRIDE_STEP
        if d == step:
            for r in range(d):
                picked = stage_ref[slot, pl.ds(r, rows_out, stride=step), :]
                out_ref[0, r, out_rows, cols] = picked.astype(BF16)
            return
        part = sub // step
        for r_lo in range(step):
            mid_ref[slot, r_lo * part:(r_lo + 1) * part, :] = stage_ref[slot, pl.ds(r_lo, part, stride=step), :]
        for r_lo in range(step):
            for r_hi in range(step):
                r = r_lo + step * r_hi
                picked = mid_ref[slot, pl.ds(r_lo * part + r_hi, rows_out, stride=step), :]
                out_ref[0, r, out_rows, cols] = picked.astype(BF16)

    def pool_mix(u, row0):
        ext = jnp.concatenate([carry_ref[...], u], axis=0)
        carry_ref[...] = u[sub - MAX_POOL_WINDOW:, :]
        head = MAX_POOL_WINDOW
        head_row = lax.broadcasted_iota(jnp.int32, (head, LANES), 0) + row0 + 1
        mixed = []
        for half in range(POOL_WIDTH // LANES):
            w_a, w_b = POOL_WINDOWS[2 * half], POOL_WINDOWS[2 * half + 1]
            acc = ext[:, half * LANES:(half + 1) * LANES]
            width = 1
            while width < w_a:
                acc = acc + pltpu.roll(acc, width, 0)
                width *= 2
            sum_a = acc
            while width < w_b:
                acc = acc + pltpu.roll(acc, width, 0)
                width *= 2
            win = jnp.where(low_group, sum_a, acc)[MAX_POOL_WINDOW:, :]
            window = jnp.where(low_group_head, w_a, w_b)
            count = jnp.minimum(head_row, window).astype(F32)
            inv_window = jnp.where(low_group_row, 1.0 / w_a, 1.0 / w_b)
            mean = jnp.concatenate([win[:head] / count, win[head:] * inv_window], axis=0)
            mixed.append(mean - u[:, half * LANES:(half + 1) * LANES])
        mixed = jnp.concatenate(mixed, axis=1).astype(BF16)
        y = jnp.dot(mixed, wpool_ref[...], preferred_element_type=F32) + bpool_ref[...]
        return (y * pscale_ref[...]).astype(BF16)

    def prologue_piece(k, dst, xsrc_ref, row0, psrc_ref, prow0, gain, shift):
        rows = slice(k * PIECE, (k + 1) * PIECE)
        x = xsrc_ref[0, row0 + k * PIECE:row0 + (k + 1) * PIECE, :]
        h_ref[dst, rows, :] = (x * _rms_scale(x) * gain + shift).astype(BF16)
        if k < sub // LANES:
            pos = psrc_ref[0, 0, prow0 + k:prow0 + k + 1, :].astype(F32)
            cos, sin = _rope_tables(pos, freq_ref[...], expand_ref[...])
            for i, table in enumerate((cos, sin, cos * q_scale, sin * q_scale)):
                tab_ref[dst, i, k * LANES:(k + 1) * LANES, :] = table

    n_sub = tm // sub
    n_pieces = sub // PIECE
    shift = mod_ref[0, 0:1, :]
    gain = g_ref[...] * (1.0 + mod_ref[0, 1:2, :])
    shift_next = modn_ref[0, 0:1, :]
    gain_next = g_ref[...] * (1.0 + modn_ref[0, 1:2, :])

    @pl.when((pl.program_id(0) == 0) & (s == 0))
    def _():
        for k in range(n_pieces):
            prologue_piece(k, 0, x_ref, 0, pos_ref, 0, gain, shift)

    chunks = [(g, kind) for g in range(len(DILATIONS)) for kind in range(3)]
    chunks.remove((0, 2))
    chunks.insert(len(chunks) // 2, (None, None))
    chunks.append((0, 2))
    slots = {}
    for g, d in enumerate(DILATIONS):
        if d > 1:
            for kind in range(3):
                for half in range(HALVES):
                    slots[g, kind, half] = len(slots)

    for a in range(n_sub):
        rows = slice(a * sub, (a + 1) * sub)
        h = h_ref[a % 2]
        cos, sin, cos_q, sin_q = (tab_ref[a % 2, i] for i in range(4))
        for ci, (g, kind) in enumerate(chunks):
            if g is None:
                raw_ref[ci % 2] = jnp.dot(h, w_ref[:, :POOL_WIDTH], preferred_element_type=F32)
                pool_ref[0, rows, :] = pool_mix(raw_ref[ci % 2], s * tm + a * sub)
            else:
                lo = POOL_WIDTH + kind * ATTN_WIDTH + g * GROUP_WIDTH
                raw_ref[ci % 2] = jnp.dot(h, w_ref[:, lo:lo + GROUP_WIDTH], preferred_element_type=F32)
                for half in range(HALVES):
                    value = raw_ref[ci % 2, :, half * LANES:(half + 1) * LANES]
                    if kind == 0:
                        value = rope(value, cos_q, sin_q)
                    elif kind == 1:
                        value = rope(value, cos, sin)
                    emit(out_refs[g][kind], value, DILATIONS[g], half, a, slots.get((g, kind, half)))
            if ci < n_pieces:
                if a + 1 < n_sub:
                    prologue_piece(ci, (a + 1) % 2, x_ref, (a + 1) * sub, pos_ref, (a + 1) * (sub // LANES),
                                   gain, shift)
                else:
                    prologue_piece(ci, (a + 1) % 2, xn_ref, 0, posn_ref, 0, gain_next, shift_next)


def _inproj_call(x, mod3, g_pre, w_in, positions, wpool_blk, bpool, pscale):
    batch, seq, _ = x.shape
    tm = INPROJ_TILE
    row = lambda b, s: (b, s, 0)
    const = lambda b, s: (0, 0)
    resident = dict(pipeline_mode=pl.Buffered(1))
    inv_freq = ROPE_THETA ** (-jnp.arange(0, ROT_DIM, 2, dtype=F32) / ROT_DIM)
    pos4 = positions.reshape(batch, seq // tm, tm // LANES, LANES)
    expand = jnp.asarray(_rope_expansion(), BF16)
    steps = seq // tm

    def next_step(b, s):
        wrap = (s + 1) // steps
        return jnp.minimum(b + wrap, batch - 1), (s + 1) % steps

    out_specs = [pl.BlockSpec((1, tm, GROUP_WIDTH), row)]
    out_shape = [jax.ShapeDtypeStruct((batch, seq, GROUP_WIDTH), BF16)]
    n_stage = 0
    for d in DILATIONS:
        out_specs += [pl.BlockSpec((1, d, tm // d, GROUP_WIDTH), lambda b, s: (b, 0, s, 0))] * 3
        out_shape += [jax.ShapeDtypeStruct((batch, d, seq // d, GROUP_WIDTH), BF16)] * 3
        n_stage += 3 * HALVES if d > 1 else 0
    return pl.pallas_call(
        _inproj_kernel,
        grid=(batch, seq // tm),
        in_specs=[pl.BlockSpec((1, tm, D_MODEL), row),
                  pl.BlockSpec((1, INPROJ_SUB, D_MODEL),
                               lambda b, s: (next_step(b, s)[0], next_step(b, s)[1] * (tm // INPROJ_SUB), 0)),
                  pl.BlockSpec((1, N_MOD, D_MODEL), lambda b, s: (b, 0, 0)),
                  pl.BlockSpec((1, N_MOD, D_MODEL), lambda b, s: (next_step(b, s)[0], 0, 0)),
                  pl.BlockSpec((1, D_MODEL), const),
                  pl.BlockSpec((D_MODEL, IN_PROJ_WIDTH), const, **resident),
                  pl.BlockSpec((1, 1, tm // LANES, LANES), lambda b, s: (b, s, 0, 0)),
                  pl.BlockSpec((1, 1, tm // LANES, LANES), lambda b, s: (*next_step(b, s), 0, 0)),
                  pl.BlockSpec((ROT_HALF, 1), const),
                  pl.BlockSpec((LANES, 2 * LANES), const),
                  pl.BlockSpec((POOL_WIDTH, POOL_WIDTH), const),
                  pl.BlockSpec((1, POOL_WIDTH), const),
                  pl.BlockSpec((1, POOL_WIDTH), const)],
        out_specs=out_specs,
        out_shape=out_shape,
        scratch_shapes=[pltpu.VMEM((MAX_POOL_WINDOW, POOL_WIDTH), F32),
                        pltpu.VMEM((n_stage, INPROJ_SUB, LANES), F32),
                        pltpu.VMEM((n_stage, INPROJ_SUB, LANES), F32),
                        pltpu.VMEM((2, INPROJ_SUB, D_MODEL), BF16),
                        pltpu.VMEM((2, 4, INPROJ_SUB, LANES), F32),
                        pltpu.VMEM((2, INPROJ_SUB, GROUP_WIDTH), F32)],
        compiler_params=_params(2),
        name="inproj",
    )(x, x, mod3, mod3, g_pre, w_in, pos4, pos4, inv_freq.reshape(ROT_HALF, 1), expand, wpool_blk, bpool,
      pscale)


def _attend(q, keys, vals, bias, head_masks):
    q_heads = jnp.concatenate([jnp.where(m, q, jnp.zeros_like(q)) for m in head_masks], axis=0)
    scores = lax.dot_general(q_heads, keys, (((1,), (1,)), ((), ())), preferred_element_type=F32)
    probs, lses = [], []
    for h in range(HEADS_PER_GROUP):
        s = scores[h * BLOCK:(h + 1) * BLOCK] + bias
        m_row = jnp.max(s, axis=-1, keepdims=True)
        p = jnp.exp(s - m_row)
        den = jnp.sum(p, axis=-1, keepdims=True)
        probs.append((p * (1.0 / den)).astype(BF16))
        lses.append(m_row + jnp.log(den))
    pv = jnp.dot(jnp.concatenate(probs, axis=0), vals, preferred_element_type=F32)
    out = pv[(HEADS_PER_GROUP - 1) * BLOCK:]
    lse_out = jnp.broadcast_to(lses[-1], (BLOCK, GROUP_WIDTH))
    for h in range(HEADS_PER_GROUP - 2, -1, -1):
        out = jnp.where(head_masks[h], pv[h * BLOCK:(h + 1) * BLOCK], out)
        lse_out = jnp.where(head_masks[h], lses[h], lse_out)
    return out, lse_out


def _attn_kernel(q0_ref, kp0_ref, kc0_ref, vp0_ref, vc0_ref,
                 q1_ref, kp1_ref, kc1_ref, vp1_ref, vc1_ref,
                 q2_ref, kp2_ref, kc2_ref, vp2_ref, vc2_ref,
                 o_ref, ost_ref, lst_ref):
    first_chunk = pl.program_id(1) == 0
    i = lax.broadcasted_iota(jnp.int32, (BLOCK, 2 * BLOCK), 0)
    j = lax.broadcasted_iota(jnp.int32, (BLOCK, 2 * BLOCK), 1)
    valid = (j >= i) & (j <= i + SPAN)
    bias = jnp.where(valid, 0.0, MASK_VALUE)
    has_prev = jnp.logical_not(first_chunk)
    bias_first = jnp.where(valid & ((j >= BLOCK) | has_prev), 0.0, MASK_VALUE)
    lane = lax.broadcasted_iota(jnp.int32, (BLOCK, GROUP_WIDTH), 1)
    head_masks = [(lane // HEAD_DIM) == h for h in range(HEADS_PER_GROUP)]
    groups = ((q0_ref, kp0_ref, kc0_ref, vp0_ref, vc0_ref),
              (q1_ref, kp1_ref, kc1_ref, vp1_ref, vc1_ref),
              (q2_ref, kp2_ref, kc2_ref, vp2_ref, vc2_ref))

    def first_tile(g, r):
        q_ref, kp_ref, kc_ref, vp_ref, vc_ref = groups[g]
        keys = jnp.concatenate([kp_ref[0, r], kc_ref[0, r, :BLOCK, :]], axis=0)
        vals = jnp.concatenate([vp_ref[0, r], vc_ref[0, r, :BLOCK, :]], axis=0)
        return _attend(q_ref[0, r, :BLOCK, :], keys, vals, bias_first, head_masks)

    def later_tile(g, r, n):
        q_ref, _, kc_ref, _, vc_ref = groups[g]
        start = (n - 1) * BLOCK
        both = pl.ds(start, 2 * BLOCK)
        return _attend(q_ref[0, r, pl.ds(start + BLOCK, BLOCK), :], kc_ref[0, r, both, :],
                       vc_ref[0, r, both, :], bias, head_masks)

    def token_rows(g, r, n):
        d = DILATIONS[g]
        start = n * (d * BLOCK) + r
        return pl.ds(start, BLOCK, stride=d) if d > 1 else pl.ds(start, BLOCK)

    def merged(rows, out, lse, half):
        cols = slice(half * LANES, (half + 1) * LANES)
        o_a, l_a, o_b, l_b = ost_ref[half, rows, :], lst_ref[half, rows, :], out[:, cols], lse[:, cols]
        top = jnp.maximum(l_a, l_b)
        w_a = jnp.exp(l_a - top)
        w_b = jnp.exp(l_b - top)
        total = w_a + w_b
        return (w_a * o_a + w_b * o_b) * (1.0 / total), top + jnp.log(total)

    def start_state(rows, out, lse):
        for half in range(HALVES):
            cols = slice(half * LANES, (half + 1) * LANES)
            ost_ref[half, rows, :] = out[:, cols]
            lst_ref[half, rows, :] = lse[:, cols]

    def fold_state(rows, out, lse):
        for half in range(HALVES):
            ost_ref[half, rows, :], lst_ref[half, rows, :] = merged(rows, out, lse, half)

    def finish(rows, out, lse):
        for half in range(HALVES):
            o_ref[0, rows, half * LANES:(half + 1) * LANES] = merged(rows, out, lse, half)[0].astype(BF16)

    blocks = [ATTN_CHUNK // (d * BLOCK) for d in DILATIONS]
    for r in range(DILATIONS[2]):
        start_state(token_rows(2, r, 0), *first_tile(2, r))
    for r in range(DILATIONS[1]):
        fold_state(token_rows(1, r, 0), *first_tile(1, r))
        for n in range(1, blocks[1]):
            fold_state(token_rows(1, r, n), *later_tile(1, r, n))
    finish(token_rows(0, 0, 0), *first_tile(0, 0))
    for n in range(1, blocks[0]):
        finish(token_rows(0, 0, n), *later_tile(0, 0, n))


def _attn_call(qkv):
    batch = qkv[0][0].shape[0]
    seq = qkv[0][0].shape[2]
    in_specs, args = [], []
    for (q, k, v), d in zip(qkv, DILATIONS):
        rows = ATTN_CHUNK // d
        cur = pl.BlockSpec((1, d, rows, GROUP_WIDTH), lambda b, c: (b, 0, c, 0))
        blocks_per_chunk = rows // BLOCK
        prev = pl.BlockSpec((1, d, BLOCK, GROUP_WIDTH),
                            lambda b, c, n=blocks_per_chunk: (b, 0, jnp.maximum(c * n - 1, 0), 0))
        in_specs += [cur, prev, cur, prev, cur]
        args += [q, k, k, v, v]
    return pl.pallas_call(
        _attn_kernel,
        grid=(batch, seq // ATTN_CHUNK),
        in_specs=in_specs,
        out_specs=pl.BlockSpec((1, ATTN_CHUNK, GROUP_WIDTH), lambda b, c: (b, c, 0)),
        out_shape=jax.ShapeDtypeStruct((batch, seq, GROUP_WIDTH), BF16),
        scratch_shapes=[pltpu.VMEM((HALVES, ATTN_CHUNK, LANES), F32),
                        pltpu.VMEM((HALVES, ATTN_CHUNK, LANES), F32)],
        compiler_params=_params(2),
        name="attn",
    )(*args)


def _ffn_kernel(x_ref, pool_ref, attn_ref, xn_ref, pooln_ref, attnn_ref, mod_ref, modn_ref,
                gpostm_ref, gpref_ref, gpostf_ref, wout_ref, wup_ref, cw_ref, cb_ref, wdown_ref,
                out_ref, act_ref, carry_ref, x1a_ref, x1b_ref, h2a_ref, h2b_ref, raw_ref):
    s = pl.program_id(1)
    tm = out_ref.shape[1]
    halo = carry_ref.shape[0]
    n_pieces = tm // FFN_PIECE

    @pl.when(s == 0)
    def _():
        carry_ref[...] = jnp.zeros_like(carry_ref)

    def prologue_piece(k, x1_dst, h2_dst, xs_ref, ps_ref, as_ref, ms_ref):
        rows = slice(k * FFN_PIECE, (k + 1) * FFN_PIECE)
        mixed = jnp.concatenate([ps_ref[0, rows, :], as_ref[0, rows, :]], axis=-1)
        y = jnp.dot(mixed, wout_ref[...], preferred_element_type=F32)
        x1 = xs_ref[0, rows, :] + ms_ref[0, 2:3, :] * (y * _rms_scale(y) * gpostm_ref[...])
        x1_dst[rows, :] = x1
        h2 = x1 * _rms_scale(x1) * gpref_ref[...] * (1.0 + ms_ref[0, 4:5, :]) + ms_ref[0, 3:4, :]
        h2_dst[rows, :] = h2.astype(BF16)

    def step(x1_cur, h2_cur, x1_next, h2_next):
        h2 = h2_cur[...]
        n_chunks = D_FF // FF_CHUNK
        for c in range(n_chunks):
            cols = slice(c * FF_CHUNK, (c + 1) * FF_CHUNK)
            vcols = slice(D_FF + c * FF_CHUNK, D_FF + (c + 1) * FF_CHUNK)
            raw_ref[c % 2, 0] = jnp.dot(h2, wup_ref[:, cols], preferred_element_type=F32)
            raw_ref[c % 2, 1] = jnp.dot(h2, wup_ref[:, vcols], preferred_element_type=F32)
            gate, val = raw_ref[c % 2, 0], raw_ref[c % 2, 1]
            ext = jnp.concatenate([carry_ref[:, cols], gate], axis=0)
            carry_ref[:, cols] = gate[tm - halo:, :]
            back1 = pltpu.roll(ext, 1, 0)[halo:, :]
            back2 = pltpu.roll(ext, 2, 0)[halo:, :]
            conv = (back2 * cw_ref[0:1, cols] + back1 * cw_ref[1:2, cols] + gate * cw_ref[2:3, cols]
                    + cb_ref[:, cols])
            act_ref[:, cols] = (jax.nn.gelu(conv, approximate=True) * val).astype(BF16)
        act = act_ref[...]
        y_parts = []
        for k in range(D_MODEL // DOWN_CHUNK):
            if k < n_pieces:
                prologue_piece(k, x1_next, h2_next, xn_ref, pooln_ref, attnn_ref, modn_ref)
            y_parts.append(jnp.dot(act, wdown_ref[:, k * DOWN_CHUNK:(k + 1) * DOWN_CHUNK],
                                   preferred_element_type=F32))
        y = jnp.concatenate(y_parts, axis=-1)
        out_ref[0] = x1_cur[...] + mod_ref[0, 5:6, :] * (y * _rms_scale(y) * gpostf_ref[...])

    @pl.when((pl.program_id(0) == 0) & (s == 0))
    def _():
        for k in range(n_pieces):
            prologue_piece(k, x1a_ref, h2a_ref, x_ref, pool_ref, attn_ref, mod_ref)

    @pl.when(s % 2 == 0)
    def _():
        step(x1a_ref, h2a_ref, x1b_ref, h2b_ref)

    @pl.when(s % 2 == 1)
    def _():
        step(x1b_ref, h2b_ref, x1a_ref, h2a_ref)


def _ffn_call(x, pool, attn, mod3, g_post_mix, g_pre_ffn, g_post_ffn, w_out, w_up, conv_w, conv_b, w_down):
    batch, seq, _ = x.shape
    tm = ROW_TILE
    steps = seq // tm
    assert steps % 2 == 0
    row = lambda b, s: (b, s, 0)
    const = lambda b, s: (0, 0)

    def nxt(b, s):
        return jnp.minimum(b + (s + 1) // steps, batch - 1), (s + 1) % steps, 0

    resident = dict(pipeline_mode=pl.Buffered(1))
    vec = pl.BlockSpec((1, D_MODEL), const)
    return pl.pallas_call(
        _ffn_kernel,
        grid=(batch, steps),
        in_specs=[pl.BlockSpec((1, tm, D_MODEL), row),
                  pl.BlockSpec((1, tm, GROUP_WIDTH), row),
                  pl.BlockSpec((1, tm, GROUP_WIDTH), row),
                  pl.BlockSpec((1, tm, D_MODEL), nxt),
                  pl.BlockSpec((1, tm, GROUP_WIDTH), nxt),
                  pl.BlockSpec((1, tm, GROUP_WIDTH), nxt),
                  pl.BlockSpec((1, N_MOD, D_MODEL), lambda b, s: (b, 0, 0)),
                  pl.BlockSpec((1, N_MOD, D_MODEL), lambda b, s: (nxt(b, s)[0], 0, 0)),
                  vec, vec, vec,
                  pl.BlockSpec((POOL_WIDTH + GROUP_WIDTH, D_MODEL), const, **resident),
                  pl.BlockSpec((D_MODEL, 2 * D_FF), const, **resident),
                  pl.BlockSpec((3, D_FF), const),
                  pl.BlockSpec((1, D_FF), const),
                  pl.BlockSpec((D_FF, D_MODEL), const, **resident)],
        out_specs=pl.BlockSpec((1, tm, D_MODEL), row),
        out_shape=jax.ShapeDtypeStruct((batch, seq, D_MODEL), F32),
        scratch_shapes=[pltpu.VMEM((tm, D_FF), BF16),
                        pltpu.VMEM((SUBLANES, D_FF), F32),
                        pltpu.VMEM((tm, D_MODEL), F32),
                        pltpu.VMEM((tm, D_MODEL), F32),
                        pltpu.VMEM((tm, D_MODEL), BF16),
                        pltpu.VMEM((tm, D_MODEL), BF16),
                        pltpu.VMEM((2, 2, tm, FF_CHUNK), F32)],
        compiler_params=_params(2),
        name="ffn",
    )(x, pool, attn, x, pool, attn, mod3, mod3, g_post_mix, g_pre_ffn, g_post_ffn, w_out, w_up, conv_w,
      conv_b, w_down)


def kernel(x, c, positions, w_ada, b_ada, g_pre_mix, g_post_mix, g_pre_ffn, g_post_ffn,
           w_in, w_pool, b_pool, pool_scale, w_out, w_up, conv_w, conv_b, w_down):
    depth = w_ada.shape[0]
    batch = x.shape[0]
    for l in range(depth):
        mod3 = _mod_call(c, w_ada[l], b_ada[l]).reshape(batch, N_MOD, D_MODEL)
        wpool_blk = jax.scipy.linalg.block_diag(*[w_pool[l, g] for g in range(len(POOL_WINDOWS))])
        pool, *qkv = _inproj_call(
            x, mod3, g_pre_mix[l].reshape(1, D_MODEL), w_in[l].astype(BF16), positions,
            wpool_blk.astype(BF16), b_pool[l].reshape(1, POOL_WIDTH), pool_scale[l].reshape(1, POOL_WIDTH))
        attn = _attn_call([qkv[0:3], qkv[3:6], qkv[6:9]])
        x = _ffn_call(x, pool, attn, mod3, g_post_mix[l].reshape(1, D_MODEL), g_pre_ffn[l].reshape(1, D_MODEL),
                      g_post_ffn[l].reshape(1, D_MODEL), w_out[l].astype(BF16), w_up[l].astype(BF16),
                      conv_w[l], conv_b[l].reshape(1, D_FF), w_down[l].astype(BF16))
    return x
```

```python
import numpy as np

import jax
import jax.numpy as jnp
from jax import lax
from jax.experimental import pallas as pl
from jax.experimental.pallas import tpu as pltpu

F32 = jnp.float32
BF16 = jnp.bfloat16

D_MODEL = 1024
HEAD_DIM = 64
POOL_WIDTH = 256
POOL_WINDOWS = (2, 4, 8, 16)
POOL_GROUP_DIM = 64
MAX_POOL_WINDOW = 16
ATTN_WIDTH = 768
GROUP_WIDTH = 256
HEADS_PER_GROUP = 4
DILATIONS = (1, 4, 16)
DESTRIDE_STEP = 4
SPAN = 128
BLOCK = 128
IN_PROJ_WIDTH = 2560
ROT_DIM = 16
ROT_HALF = 8
ROPE_THETA = 500000.0
D_FF = 2816
NORM_EPS = 1e-6
N_MOD = 6
MASK_VALUE = -1e30

LANES = 128
SUBLANES = 8
VMEM_LIMIT_BYTES = 56 * 1024 * 1024

ROW_TILE = 512
FFN_PIECE = 128
DOWN_CHUNK = 256
INPROJ_TILE = 1024
INPROJ_SUB = 512
PIECE = 64
FF_CHUNK = 256
ATTN_CHUNK = BLOCK * DILATIONS[-1]
HALVES = GROUP_WIDTH // LANES
N_SPLIT = 3


def _params(n_axes, vmem=VMEM_LIMIT_BYTES):
    return pltpu.CompilerParams(dimension_semantics=("arbitrary",) * n_axes, vmem_limit_bytes=vmem)


def _rms_scale(v):
    return lax.rsqrt(jnp.mean(v * v, axis=-1, keepdims=True) + NORM_EPS)


def _mod_kernel(c_ref, w_ref, b_ref, o_ref):
    c = c_ref[...]
    c_act = c * jax.nn.sigmoid(c)
    o_ref[...] = jnp.dot(c_act.astype(BF16), w_ref[...].astype(BF16),
                         preferred_element_type=F32) + b_ref[...]


def _mod_call(c, w_ada, b_ada):
    batch = c.shape[0]
    n = w_ada.shape[1]
    tn = 768
    return pl.pallas_call(
        _mod_kernel,
        grid=(n // tn,),
        in_specs=[pl.BlockSpec((batch, D_MODEL), lambda j: (0, 0)),
                  pl.BlockSpec((D_MODEL, tn), lambda j: (0, j)),
                  pl.BlockSpec((1, tn), lambda j: (0, j))],
        out_specs=pl.BlockSpec((batch, tn), lambda j: (0, j)),
        out_shape=jax.ShapeDtypeStruct((batch, n), F32),
        compiler_params=_params(1),
        name="mod",
    )(c, w_ada, b_ada.reshape(1, n))


def _qk_lane_layout():
    rest = HEAD_DIM - ROT_DIM
    source, owner = [], []
    for t_half in range(2):
        for head in range(2):
            source += [head * HEAD_DIM + t_half * ROT_HALF + i for i in range(ROT_HALF)]
            owner += [head] * ROT_HALF
        source += [t_half * HEAD_DIM + ROT_DIM + i for i in range(rest)]
        owner += [t_half] * rest
    return np.array(source), np.array(owner)


def _w_in_columns():
    source, _ = _qk_lane_layout()
    cols = np.arange(IN_PROJ_WIDTH)
    for start in range(POOL_WIDTH, POOL_WIDTH + 2 * ATTN_WIDTH, LANES):
        cols[start:start + LANES] = start + source
    return cols


def _rope_expansion():
    e = np.zeros((LANES, 2 * LANES), np.float32)
    ones_row = 2 * N_SPLIT * ROT_HALF
    source, _ = _qk_lane_layout()
    for lane in range(LANES):
        dim = source[lane] % HEAD_DIM
        if dim >= ROT_DIM:
            e[ones_row, lane] = 1.0
            continue
        sign = -1.0 if dim < ROT_HALF else 1.0
        for t in range(N_SPLIT):
            e[t * ROT_HALF + dim % ROT_HALF, lane] = 1.0
            e[(N_SPLIT + t) * ROT_HALF + dim % ROT_HALF, LANES + lane] = sign
    return e


def _rope_tables(pos_rows, freq_col, expand):
    cos_parts, sin_parts = [], []
    for j in range(pos_rows.shape[0]):
        ang = freq_col * pos_rows[j:j + 1, :]
        terms = []
        for table in (jnp.cos(ang), jnp.sin(ang)):
            rest = table
            for _ in range(N_SPLIT):
                part = rest.astype(BF16).astype(F32)
                terms.append(part)
                rest = rest - part
        terms.append(jnp.ones((SUBLANES, LANES), F32))
        pad = jnp.zeros((LANES - SUBLANES * len(terms), LANES), F32)
        stacked = jnp.concatenate(terms + [pad], axis=0)
        both = jnp.dot(stacked.T.astype(BF16), expand, preferred_element_type=F32)
        cos_parts.append(both[:, :LANES])
        sin_parts.append(both[:, LANES:])
    return jnp.concatenate(cos_parts, axis=0), jnp.concatenate(sin_parts, axis=0)


def _inproj_kernel(x_ref, xn_ref, mod_ref, modn_ref, g_ref, w_ref, pos_ref, posn_ref, freq_ref, expand_ref,
                   wpool_ref, bpool_ref, pscale_ref,
                   pool_ref, q0_ref, k0_ref, v0_ref, q1_ref, k1_ref, v1_ref, q2_ref, k2_ref, v2_ref,
                   carry_ref, stage_ref, mid_ref, h_ref, tab_ref, raw_ref):
    s = pl.program_id(1)
    tm = x_ref.shape[1]
    sub = INPROJ_SUB
    q_scale = HEAD_DIM ** -0.5
    low_group =lax.broadcasted_iota(jnp.int32, (sub + MAX_POOL_WINDOW, LANES), 1) < POOL_GROUP_DIM
    low_group_head = lax.broadcasted_iota(jnp.int32, (MAX_POOL_WINDOW, LANES), 1) < POOL_GROUP_DIM
    low_group_row = lax.broadcasted_iota(jnp.int32, (1, LANES), 1) < POOL_GROUP_DIM
    out_refs = ((q0_ref, k0_ref, v0_ref), (q1_ref, k1_ref, v1_ref), (q2_ref, k2_ref, v2_ref))

    @pl.when(s == 0)
    def _():
        carry_ref[...] = jnp.zeros_like(carry_ref)

    def rope(t, cos, sin):
        return t * cos + pltpu.roll(t, LANES // 2, 1) * sin

    def emit(out_ref, value, d, half, a, slot):
        cols = slice(half * LANES, (half + 1) * LANES)
        rows_out = sub // d
        out_rows = slice(a * rows_out, (a + 1) * rows_out)
        if d == 1:
            out_ref[0, 0, out_rows, cols] = value.astype(BF16)
            return
        stage_ref[slot] = value
        step = DESTRIDE_STEP
        if d == step:
            for r in range(d):
                picked = stage_ref[slot, pl.ds(r, rows_out, stride=step), :]
                out_ref[0, r, out_rows, cols] = picked.astype(BF16)
            return
        part = sub // step
        for r_lo in range(step):
            mid_ref[slot, r_lo * part:(r_lo + 1) * part, :] = stage_ref[slot, pl.ds(r_lo, part, stride=step), :]
        for r_lo in range(step):
            for r_hi in range(step):
                r = r_lo + step * r_hi
                picked = mid_ref[slot, pl.ds(r_lo * part + r_hi, rows_out, stride=step), :]
                out_ref[0, r, out_rows, cols] = picked.astype(BF16)

    def pool_mix(u, row0):
        ext = jnp.concatenate([carry_ref[...], u], axis=0)
        carry_ref[...] = u[sub - MAX_POOL_WINDOW:, :]
        head = MAX_POOL_WINDOW
        head_row = lax.broadcasted_iota(jnp.int32, (head, LANES), 0) + row0 + 1
        mixed = []
        for half in range(POOL_WIDTH // LANES):
            w_a, w_b = POOL_WINDOWS[2 * half], POOL_WINDOWS[2 * half + 1]
            acc = ext[:, half * LANES:(half + 1) * LANES]
            width = 1
            while width < w_a:
                acc = acc + pltpu.roll(acc, width, 0)
                width *= 2
            sum_a = acc
            while width < w_b:
                acc = acc + pltpu.roll(acc, width, 0)
                width *= 2
            win = jnp.where(low_group, sum_a, acc)[MAX_POOL_WINDOW:, :]
            window = jnp.where(low_group_head, w_a, w_b)
            count = jnp.minimum(head_row, window).astype(F32)
            inv_window = jnp.where(low_group_row, 1.0 / w_a, 1.0 / w_b)
            mean = jnp.concatenate([win[:head] / count, win[head:] * inv_window], axis=0)
            mixed.append(mean - u[:, half * LANES:(half + 1) * LANES])
        mixed = jnp.concatenate(mixed, axis=1).astype(BF16)
        y = jnp.dot(mixed, wpool_ref[...], preferred_element_type=F32) + bpool_ref[...]
        return (y * pscale_ref[...]).astype(BF16)

    def prologue_piece(k, dst, xsrc_ref, row0, psrc_ref, prow0, gain, shift):
        rows = slice(k * PIECE, (k + 1) * PIECE)
        x = xsrc_ref[0, row0 + k * PIECE:row0 + (k + 1) * PIECE, :]
        h_ref[dst, rows, :] = (x * _rms_scale(x) * gain + shift).astype(BF16)
        if k < sub // LANES:
            pos = psrc_ref[0, 0, prow0 + k:prow0 + k + 1, :].astype(F32)
            cos, sin = _rope_tables(pos, freq_ref[...], expand_ref[...])
            for i, table in enumerate((cos, sin, cos * q_scale, sin * q_scale)):
                tab_ref[dst, i, k * LANES:(k + 1) * LANES, :] = table

    n_sub = tm // sub
    n_pieces = sub // PIECE
    shift = mod_ref[0, 0:1, :]
    gain = g_ref[...] * (1.0 + mod_ref[0, 1:2, :])
    shift_next = modn_ref[0, 0:1, :]
    gain_next = g_ref[...] * (1.0 + modn_ref[0, 1:2, :])

    @pl.when((pl.program_id(0) == 0) & (s == 0))
    def _():
        for k in range(n_pieces):
            prologue_piece(k, 0, x_ref, 0, pos_ref, 0, gain, shift)

    chunks = [(g, kind) for g in range(len(DILATIONS)) for kind in range(3)]
    chunks.remove((0, 2))
    chunks.insert(len(chunks) // 2, (None, None))
    chunks.append((0, 2))
    slots = {}
    for g, d in enumerate(DILATIONS):
        if d > 1:
            for kind in range(3):
                for half in range(HALVES):
                    slots[g, kind, half] = len(slots)

    for a in range(n_sub):
        rows = slice(a * sub, (a + 1) * sub)
        h = h_ref[a % 2]
        cos, sin, cos_q, sin_q = (tab_ref[a % 2, i] for i in range(4))
        for ci, (g, kind) in enumerate(chunks):
            if g is None:
                raw_ref[ci % 2] = jnp.dot(h, w_ref[:, :POOL_WIDTH], preferred_element_type=F32)
                pool_ref[0, rows, :] = pool_mix(raw_ref[ci % 2], s * tm + a * sub)
            else:
                lo = POOL_WIDTH + kind * ATTN_WIDTH + g * GROUP_WIDTH
                raw_ref[ci % 2] = jnp.dot(h, w_ref[:, lo:lo + GROUP_WIDTH], preferred_element_type=F32)
                for half in range(HALVES):
                    value = raw_ref[ci % 2, :, half * LANES:(half + 1) * LANES]
                    if kind == 0:
                        value = rope(value, cos_q, sin_q)
                    elif kind == 1:
                        value = rope(value, cos, sin)
                    emit(out_refs[g][kind], value, DILATIONS[g], half, a, slots.get((g, kind, half)))
            if ci < n_pieces:
                if a + 1 < n_sub:
                    prologue_piece(ci, (a + 1) % 2, x_ref, (a + 1) * sub, pos_ref, (a + 1) * (sub // LANES),
                                   gain, shift)
                else:
                    prologue_piece(ci, (a + 1) % 2, xn_ref, 0, posn_ref, 0, gain_next, shift_next)


def _inproj_call(x, mod3, g_pre, w_in, positions, wpool_blk, bpool, pscale):
    batch, seq, _ = x.shape
    tm = INPROJ_TILE
    row = lambda b, s: (b, s, 0)
    const = lambda b, s: (0, 0)
    resident = dict(pipeline_mode=pl.Buffered(1))
    inv_freq = ROPE_THETA ** (-jnp.arange(0, ROT_DIM, 2, dtype=F32) / ROT_DIM)
    pos4 = positions.reshape(batch, seq // tm, tm // LANES, LANES)
    expand = jnp.asarray(_rope_expansion(), BF16)
    steps = seq // tm

    def next_step(b, s):
        wrap = (s + 1) // steps
        return jnp.minimum(b + wrap, batch - 1), (s + 1) % steps

    out_specs = [pl.BlockSpec((1, tm, GROUP_WIDTH), row)]
    out_shape = [jax.ShapeDtypeStruct((batch, seq, GROUP_WIDTH), BF16)]
    n_stage = 0
    for d in DILATIONS:
        out_specs += [pl.BlockSpec((1, d, tm // d, GROUP_WIDTH), lambda b, s: (b, 0, s, 0))] * 3
        out_shape += [jax.ShapeDtypeStruct((batch, d, seq // d, GROUP_WIDTH), BF16)] * 3
        n_stage += 3 * HALVES if d > 1 else 0
    return pl.pallas_call(
        _inproj_kernel,
        grid=(batch, seq // tm),
        in_specs=[pl.BlockSpec((1, tm, D_MODEL), row),
                  pl.BlockSpec((1, INPROJ_SUB, D_MODEL),
                               lambda b, s: (next_step(b, s)[0], next_step(b, s)[1] * (tm // INPROJ_SUB), 0)),
                  pl.BlockSpec((1, N_MOD, D_MODEL), lambda b, s: (b, 0, 0)),
                  pl.BlockSpec((1, N_MOD, D_MODEL), lambda b, s: (next_step(b, s)[0], 0, 0)),
                  pl.BlockSpec((1, D_MODEL), const),
                  pl.BlockSpec((D_MODEL, IN_PROJ_WIDTH), const, **resident),
                  pl.BlockSpec((1, 1, tm // LANES, LANES), lambda b, s: (b, s, 0, 0)),
                  pl.BlockSpec((1, 1, tm // LANES, LANES), lambda b, s: (*next_step(b, s), 0, 0)),
                  pl.BlockSpec((ROT_HALF, 1), const),
                  pl.BlockSpec((LANES, 2 * LANES), const),
                  pl.BlockSpec((POOL_WIDTH, POOL_WIDTH), const),
                  pl.BlockSpec((1, POOL_WIDTH), const),
                  pl.BlockSpec((1, POOL_WIDTH), const)],
        out_specs=out_specs,
        out_shape=out_shape,
        scratch_shapes=[pltpu.VMEM((MAX_POOL_WINDOW, POOL_WIDTH), F32),
                        pltpu.VMEM((n_stage, INPROJ_SUB, LANES), F32),
                        pltpu.VMEM((n_stage, INPROJ_SUB, LANES), F32),
                        pltpu.VMEM((2, INPROJ_SUB, D_MODEL), BF16),
                        pltpu.VMEM((2, 4, INPROJ_SUB, LANES), F32),
                        pltpu.VMEM((2, INPROJ_SUB, GROUP_WIDTH), F32)],
        compiler_params=_params(2),
        name="inproj",
    )(x, x, mod3, mod3, g_pre, w_in, pos4, pos4, inv_freq.reshape(ROT_HALF, 1), expand, wpool_blk, bpool,
      pscale)


def _attend(q, keys, vals, bias, head_masks):
    qk_masks, head_masks = head_masks
    q_heads = jnp.concatenate([jnp.where(m, q, jnp.zeros_like(q)) for m in qk_masks], axis=0)
    scores = lax.dot_general(q_heads, keys, (((1,), (1,)), ((), ())), preferred_element_type=F32)
    probs, lses = [], []
    for h in range(HEADS_PER_GROUP):
        s = scores[h * BLOCK:(h + 1) * BLOCK] + bias
        m_row = jnp.max(s, axis=-1, keepdims=True)
        p = jnp.exp(s - m_row)
        den = jnp.sum(p, axis=-1, keepdims=True)
        probs.append((p * (1.0 / den)).astype(BF16))
        lses.append(m_row + jnp.log(den))
    pv = jnp.dot(jnp.concatenate(probs, axis=0), vals, preferred_element_type=F32)
    out = pv[(HEADS_PER_GROUP - 1) * BLOCK:]
    lse_out = jnp.broadcast_to(lses[-1], (BLOCK, GROUP_WIDTH))
    for h in range(HEADS_PER_GROUP - 2, -1, -1):
        out = jnp.where(head_masks[h], pv[h * BLOCK:(h + 1) * BLOCK], out)
        lse_out = jnp.where(head_masks[h], lses[h], lse_out)
    return out, lse_out


def _attn_kernel(q0_ref, kp0_ref, kc0_ref, vp0_ref, vc0_ref,
                 q1_ref, kp1_ref, kc1_ref, vp1_ref, vc1_ref,
                 q2_ref, kp2_ref, kc2_ref, vp2_ref, vc2_ref,
                 o_ref, ost_ref, lst_ref):
    first_chunk = pl.program_id(1) == 0
    i = lax.broadcasted_iota(jnp.int32, (BLOCK, 2 * BLOCK), 0)
    j = lax.broadcasted_iota(jnp.int32, (BLOCK, 2 * BLOCK), 1)
    valid = (j >= i) & (j <= i + SPAN)
    bias = jnp.where(valid, 0.0, MASK_VALUE)
    has_prev = jnp.logical_not(first_chunk)
    bias_first = jnp.where(valid & ((j >= BLOCK) | has_prev), 0.0, MASK_VALUE)
    lane = lax.broadcasted_iota(jnp.int32, (BLOCK, GROUP_WIDTH), 1)
    v_masks = [(lane // HEAD_DIM) == h for h in range(HEADS_PER_GROUP)]
    _, owner = _qk_lane_layout()
    edges = [0] + [i for i in range(1, LANES) if owner[i] != owner[i - 1]] + [LANES]
    in_block = lane % LANES
    first_owner = jnp.zeros((BLOCK, GROUP_WIDTH), jnp.bool_)
    for lo, hi in zip(edges[:-1], edges[1:]):
        if owner[lo] == 0:
            first_owner = first_owner | ((in_block >= lo) & (in_block < hi))
    qk_masks = []
    for h in range(HEADS_PER_GROUP):
        in_pair = (lane // LANES) == h // 2
        qk_masks.append(in_pair & (first_owner if h % 2 == 0 else jnp.logical_not(first_owner)))
    head_masks = (qk_masks, v_masks)
    groups = ((q0_ref, kp0_ref, kc0_ref, vp0_ref, vc0_ref),
              (q1_ref, kp1_ref, kc1_ref, vp1_ref, vc1_ref),
              (q2_ref, kp2_ref, kc2_ref, vp2_ref, vc2_ref))

    def first_tile(g, r):
        q_ref, kp_ref, kc_ref, vp_ref, vc_ref = groups[g]
        keys = jnp.concatenate([kp_ref[0, r], kc_ref[0, r, :BLOCK, :]], axis=0)
        vals = jnp.concatenate([vp_ref[0, r], vc_ref[0, r, :BLOCK, :]], axis=0)
        return _attend(q_ref[0, r, :BLOCK, :], keys, vals, bias_first, head_masks)

    def later_tile(g, r, n):
        q_ref, _, kc_ref, _, vc_ref = groups[g]
        start = (n - 1) * BLOCK
        both = pl.ds(start, 2 * BLOCK)
        return _attend(q_ref[0, r, pl.ds(start + BLOCK, BLOCK), :], kc_ref[0, r, both, :],
                       vc_ref[0, r, both, :], bias, head_masks)

    def token_rows(g, r, n):
        d = DILATIONS[g]
        start = n * (d * BLOCK) + r
        return pl.ds(start, BLOCK, stride=d) if d > 1 else pl.ds(start, BLOCK)

    def merged(rows, out, lse, half):
        cols = slice(half * LANES, (half + 1) * LANES)
        o_a, l_a, o_b, l_b = ost_ref[half, rows, :], lst_ref[half, rows, :], out[:, cols], lse[:, cols]
        top = jnp.maximum(l_a, l_b)
        w_a = jnp.exp(l_a - top)
        w_b = jnp.exp(l_b - top)
        total = w_a + w_b
        return (w_a * o_a + w_b * o_b) * (1.0 / total), top + jnp.log(total)

    def start_state(rows, out, lse):
        for half in range(HALVES):
            cols = slice(half * LANES, (half + 1) * LANES)
            ost_ref[half, rows, :] = out[:, cols]
            lst_ref[half, rows, :] = lse[:, cols]

    def fold_state(rows, out, lse):
        for half in range(HALVES):
            ost_ref[half, rows, :], lst_ref[half, rows, :] = merged(rows, out, lse, half)

    def finish(rows, out, lse):
        for half in range(HALVES):
            o_ref[0, rows, half * LANES:(half + 1) * LANES] = merged(rows, out, lse, half)[0].astype(BF16)

    blocks = [ATTN_CHUNK // (d * BLOCK) for d in DILATIONS]
    for r in range(DILATIONS[2]):
        start_state(token_rows(2, r, 0), *first_tile(2, r))
    for r in range(DILATIONS[1]):
        fold_state(token_rows(1, r, 0), *first_tile(1, r))
        for n in range(1, blocks[1]):
            fold_state(token_rows(1, r, n), *later_tile(1, r, n))
    finish(token_rows(0, 0, 0), *first_tile(0, 0))
    for n in range(1, blocks[0]):
        finish(token_rows(0, 0, n), *later_tile(0, 0, n))


def _attn_call(qkv):
    batch = qkv[0][0].shape[0]
    seq = qkv[0][0].shape[2]
    in_specs, args = [], []
    for (q, k, v), d in zip(qkv, DILATIONS):
        rows = ATTN_CHUNK // d
        cur = pl.BlockSpec((1, d, rows, GROUP_WIDTH), lambda b, c: (b, 0, c, 0))
        blocks_per_chunk = rows // BLOCK
        prev = pl.BlockSpec((1, d, BLOCK, GROUP_WIDTH),
                            lambda b, c, n=blocks_per_chunk: (b, 0, jnp.maximum(c * n - 1, 0), 0))
        in_specs += [cur, prev, cur, prev, cur]
        args += [q, k, k, v, v]
    return pl.pallas_call(
        _attn_kernel,
        grid=(batch, seq // ATTN_CHUNK),
        in_specs=in_specs,
        out_specs=pl.BlockSpec((1, ATTN_CHUNK, GROUP_WIDTH), lambda b, c: (b, c, 0)),
        out_shape=jax.ShapeDtypeStruct((batch, seq, GROUP_WIDTH), BF16),
        scratch_shapes=[pltpu.VMEM((HALVES, ATTN_CHUNK, LANES), F32),
                        pltpu.VMEM((HALVES, ATTN_CHUNK, LANES), F32)],
        compiler_params=_params(2),
        name="attn",
    )(*args)


def _ffn_kernel(x_ref, pool_ref, attn_ref, xn_ref, pooln_ref, attnn_ref, mod_ref, modn_ref,
                gpostm_ref, gpref_ref, gpostf_ref, wout_ref, wup_ref, cw_ref, cb_ref, wdown_ref,
                out_ref, act_ref, carry_ref, x1a_ref, x1b_ref, h2a_ref, h2b_ref, raw_ref):
    s = pl.program_id(1)
    tm = out_ref.shape[1]
    halo = carry_ref.shape[0]
    n_pieces = tm // FFN_PIECE

    @pl.when(s == 0)
    def _():
        carry_ref[...] = jnp.zeros_like(carry_ref)

    def prologue_piece(k, x1_dst, h2_dst, xs_ref, ps_ref, as_ref, ms_ref):
        rows = slice(k * FFN_PIECE, (k + 1) * FFN_PIECE)
        mixed = jnp.concatenate([ps_ref[0, rows, :], as_ref[0, rows, :]], axis=-1)
        y = jnp.dot(mixed, wout_ref[...], preferred_element_type=F32)
        x1 = xs_ref[0, rows, :] + ms_ref[0, 2:3, :] * (y * _rms_scale(y) * gpostm_ref[...])
        x1_dst[rows, :] = x1
        h2 = x1 * _rms_scale(x1) * gpref_ref[...] * (1.0 + ms_ref[0, 4:5, :]) + ms_ref[0, 3:4, :]
        h2_dst[rows, :] = h2.astype(BF16)

    def step(x1_cur, h2_cur, x1_next, h2_next):
        h2 = h2_cur[...]
        n_chunks = D_FF // FF_CHUNK
        for c in range(n_chunks):
            cols = slice(c * FF_CHUNK, (c + 1) * FF_CHUNK)
            vcols = slice(D_FF + c * FF_CHUNK, D_FF + (c + 1) * FF_CHUNK)
            raw_ref[c % 2, 0] = jnp.dot(h2, wup_ref[:, cols], preferred_element_type=F32)
            raw_ref[c % 2, 1] = jnp.dot(h2, wup_ref[:, vcols], preferred_element_type=F32)
            gate, val = raw_ref[c % 2, 0], raw_ref[c % 2, 1]
            ext = jnp.concatenate([carry_ref[:, cols], gate], axis=0)
            carry_ref[:, cols] = gate[tm - halo:, :]
            back1 = pltpu.roll(ext, 1, 0)[halo:, :]
            back2 = pltpu.roll(ext, 2, 0)[halo:, :]
            conv = (back2 * cw_ref[0:1, cols] + back1 * cw_ref[1:2, cols] + gate * cw_ref[2:3, cols]
                    + cb_ref[:, cols])
            act_ref[:, cols] = (jax.nn.gelu(conv, approximate=True) * val).astype(BF16)
        act = act_ref[...]
        y_parts = []
        for k in range(D_MODEL // DOWN_CHUNK):
            if k < n_pieces:
                prologue_piece(k, x1_next, h2_next, xn_ref, pooln_ref, attnn_ref, modn_ref)
            y_parts.append(jnp.dot(act, wdown_ref[:, k * DOWN_CHUNK:(k + 1) * DOWN_CHUNK],
                                   preferred_element_type=F32))
        y = jnp.concatenate(y_parts, axis=-1)
        out_ref[0] = x1_cur[...] + mod_ref[0, 5:6, :] * (y * _rms_scale(y) * gpostf_ref[...])

    @pl.when((pl.program_id(0) == 0) & (s == 0))
    def _():
        for k in range(n_pieces):
            prologue_piece(k, x1a_ref, h2a_ref, x_ref, pool_ref, attn_ref, mod_ref)

    @pl.when(s % 2 == 0)
    def _():
        step(x1a_ref, h2a_ref, x1b_ref, h2b_ref)

    @pl.when(s % 2 == 1)
    def _():
        step(x1b_ref, h2b_ref, x1a_ref, h2a_ref)


def _ffn_call(x, pool, attn, mod3, g_post_mix, g_pre_ffn, g_post_ffn, w_out, w_up, conv_w, conv_b, w_down):
    batch, seq, _ = x.shape
    tm = ROW_TILE
    steps = seq // tm
    assert steps % 2 == 0
    row = lambda b, s: (b, s, 0)
    const = lambda b, s: (0, 0)

    def nxt(b, s):
        return jnp.minimum(b + (s + 1) // steps, batch - 1), (s + 1) % steps, 0

    resident = dict(pipeline_mode=pl.Buffered(1))
    vec = pl.BlockSpec((1, D_MODEL), const)
    return pl.pallas_call(
        _ffn_kernel,
        grid=(batch, steps),
        in_specs=[pl.BlockSpec((1, tm, D_MODEL), row),
                  pl.BlockSpec((1, tm, GROUP_WIDTH), row),
                  pl.BlockSpec((1, tm, GROUP_WIDTH), row),
                  pl.BlockSpec((1, tm, D_MODEL), nxt),
                  pl.BlockSpec((1, tm, GROUP_WIDTH), nxt),
                  pl.BlockSpec((1, tm, GROUP_WIDTH), nxt),
                  pl.BlockSpec((1, N_MOD, D_MODEL), lambda b, s: (b, 0, 0)),
                  pl.BlockSpec((1, N_MOD, D_MODEL), lambda b, s: (nxt(b, s)[0], 0, 0)),
                  vec, vec, vec,
                  pl.BlockSpec((POOL_WIDTH + GROUP_WIDTH, D_MODEL), const, **resident),
                  pl.BlockSpec((D_MODEL, 2 * D_FF), const, **resident),
                  pl.BlockSpec((3, D_FF), const),
                  pl.BlockSpec((1, D_FF), const),
                  pl.BlockSpec((D_FF, D_MODEL), const, **resident)],
        out_specs=pl.BlockSpec((1, tm, D_MODEL), row),
        out_shape=jax.ShapeDtypeStruct((batch, seq, D_MODEL), F32),
        scratch_shapes=[pltpu.VMEM((tm, D_FF), BF16),
                        pltpu.VMEM((SUBLANES, D_FF), F32),
                        pltpu.VMEM((tm, D_MODEL), F32),
                        pltpu.VMEM((tm, D_MODEL), F32),
                        pltpu.VMEM((tm, D_MODEL), BF16),
                        pltpu.VMEM((tm, D_MODEL), BF16),
                        pltpu.VMEM((2, 2, tm, FF_CHUNK), F32)],
        compiler_params=_params(2),
        name="ffn",
    )(x, pool, attn, x, pool, attn, mod3, mod3, g_post_mix, g_pre_ffn, g_post_ffn, w_out, w_up, conv_w,
      conv_b, w_down)


def kernel(x, c, positions, w_ada, b_ada, g_pre_mix, g_post_mix, g_pre_ffn, g_post_ffn,
           w_in, w_pool, b_pool, pool_scale, w_out, w_up, conv_w, conv_b, w_down):
    depth = w_ada.shape[0]
    batch = x.shape[0]
    for l in range(depth):
        mod3 = _mod_call(c, w_ada[l], b_ada[l]).reshape(batch, N_MOD, D_MODEL)
        wpool_blk = jax.scipy.linalg.block_diag(*[w_pool[l, g] for g in range(len(POOL_WINDOWS))])
        pool, *qkv = _inproj_call(
            x, mod3, g_pre_mix[l].reshape(1, D_MODEL), w_in[l][:, _w_in_columns()].astype(BF16), positions,
            wpool_blk.astype(BF16), b_pool[l].reshape(1, POOL_WIDTH), pool_scale[l].reshape(1, POOL_WIDTH))
        attn = _attn_call([qkv[0:3], qkv[3:6], qkv[6:9]])
        x = _ffn_call(x, pool, attn, mod3, g_post_mix[l].reshape(1, D_MODEL), g_pre_ffn[l].reshape(1, D_MODEL),
                      g_post_ffn[l].reshape(1, D_MODEL), w_out[l].astype(BF16), w_up[l].astype(BF16),
                      conv_w[l], conv_b[l].reshape(1, D_FF), w_down[l].astype(BF16))
    return x
```

```python
import functools

import numpy as np

import jax
import jax.numpy as jnp
from jax import lax
from jax.experimental import pallas as pl
from jax.experimental.pallas import tpu as pltpu

F32 = jnp.float32
BF16 = jnp.bfloat16

D_MODEL = 1024
HEAD_DIM = 64
POOL_WIDTH = 256
POOL_WINDOWS = (2, 4, 8, 16)
POOL_GROUP_DIM = 64
MAX_POOL_WINDOW = 16
ATTN_WIDTH = 768
GROUP_WIDTH = 256
HEADS_PER_GROUP = 4
DILATIONS = (1, 4, 16)
DESTRIDE_STEP = 4
SPAN = 128
BLOCK = 128
IN_PROJ_WIDTH = 2560
ROT_DIM = 16
ROT_HALF = 8
ROPE_THETA = 500000.0
D_FF = 2816
NORM_EPS = 1e-6
N_MOD = 6
MASK_VALUE = -1e30

LANES = 128
SUBLANES = 8
VMEM_LIMIT_BYTES = 56 * 1024 * 1024

ROW_TILE = 512
FFN_PIECE = 128
DOWN_CHUNK = 256
INPROJ_TILE = 1024
INPROJ_SUB = 512
PIECE = 64
FF_CHUNK = 256
ATTN_CHUNK = BLOCK * DILATIONS[-1]
HALVES = GROUP_WIDTH // LANES
N_SPLIT = 3


def _params(n_axes, vmem=VMEM_LIMIT_BYTES):
    return pltpu.CompilerParams(dimension_semantics=("arbitrary",) * n_axes, vmem_limit_bytes=vmem)


def _rms_scale(v):
    return lax.rsqrt(jnp.mean(v * v, axis=-1, keepdims=True) + NORM_EPS)


def _mod_kernel(c_ref, w_ref, b_ref, o_ref):
    c = c_ref[...]
    c_act = c * jax.nn.sigmoid(c)
    o_ref[...] = jnp.dot(c_act.astype(BF16), w_ref[...].astype(BF16),
                         preferred_element_type=F32) + b_ref[...]


def _mod_call(c, w_ada, b_ada):
    batch = c.shape[0]
    n = w_ada.shape[1]
    tn = 768
    return pl.pallas_call(
        _mod_kernel,
        grid=(n // tn,),
        in_specs=[pl.BlockSpec((batch, D_MODEL), lambda j: (0, 0)),
                  pl.BlockSpec((D_MODEL, tn), lambda j: (0, j)),
                  pl.BlockSpec((1, tn), lambda j: (0, j))],
        out_specs=pl.BlockSpec((batch, tn), lambda j: (0, j)),
        out_shape=jax.ShapeDtypeStruct((batch, n), F32),
        compiler_params=_params(1),
        name="mod",
    )(c, w_ada, b_ada.reshape(1, n))


def _rope_expansion():
    e = np.zeros((LANES, 2 * LANES), np.float32)
    ones_row = 2 * N_SPLIT * ROT_HALF
    for lane in range(LANES):
        in_head = lane % HEAD_DIM
        if in_head >= ROT_DIM:
            e[ones_row, lane] = 1.0
            continue
        sign = -1.0 if in_head < ROT_HALF else 1.0
        for t in range(N_SPLIT):
            e[t * ROT_HALF + in_head % ROT_HALF, lane] = 1.0
            e[(N_SPLIT + t) * ROT_HALF + in_head % ROT_HALF, LANES + lane] = sign
    return e


def _rope_tables(pos_rows, freq_col, expand):
    cos_parts, sin_parts = [], []
    for j in range(pos_rows.shape[0]):
        ang = freq_col * pos_rows[j:j + 1, :]
        terms = []
        for table in (jnp.cos(ang), jnp.sin(ang)):
            rest = table
            for _ in range(N_SPLIT):
                part = rest.astype(BF16).astype(F32)
                terms.append(part)
                rest = rest - part
        terms.append(jnp.ones((SUBLANES, LANES), F32))
        pad = jnp.zeros((LANES - SUBLANES * len(terms), LANES), F32)
        stacked = jnp.concatenate(terms + [pad], axis=0)
        both = jnp.dot(stacked.T.astype(BF16), expand, preferred_element_type=F32)
        cos_parts.append(both[:, :LANES])
        sin_parts.append(both[:, LANES:])
    return jnp.concatenate(cos_parts, axis=0), jnp.concatenate(sin_parts, axis=0)


def _inproj_kernel(x_ref, xn_ref, mod_ref, modn_ref, g_ref, w_ref, pos_ref, posn_ref, freq_ref, expand_ref,
                   wpool_ref, bpool_ref, pscale_ref,
                   pool_ref, q0_ref, k0_ref, v0_ref, q1_ref, k1_ref, v1_ref, q2_ref, k2_ref, v2_ref,
                   carry_ref, stage_ref, mid_ref, h_ref, tab_ref, raw_ref):
    s = pl.program_id(1)
    tm = x_ref.shape[1]
    sub = INPROJ_SUB
    q_scale = HEAD_DIM ** -0.5
    first_half = (lax.broadcasted_iota(jnp.int32, (sub, LANES), 1) % HEAD_DIM) < ROT_HALF
    low_group = lax.broadcasted_iota(jnp.int32, (sub + MAX_POOL_WINDOW, LANES), 1) < POOL_GROUP_DIM
    low_group_head = lax.broadcasted_iota(jnp.int32, (MAX_POOL_WINDOW, LANES), 1) < POOL_GROUP_DIM
    low_group_row = lax.broadcasted_iota(jnp.int32, (1, LANES), 1) < POOL_GROUP_DIM
    out_refs = ((q0_ref, k0_ref, v0_ref), (q1_ref, k1_ref, v1_ref), (q2_ref, k2_ref, v2_ref))

    @pl.when(s == 0)
    def _():
        carry_ref[...] = jnp.zeros_like(carry_ref)

    def rope(t, cos, sin):
        partner = jnp.where(first_half, pltpu.roll(t, LANES - ROT_HALF, 1), pltpu.roll(t, ROT_HALF, 1))
        return t * cos + partner * sin

    def emit(out_ref, value, d, half, a, slot):
        cols = slice(half * LANES, (half + 1) * LANES)
        rows_out = sub // d
        out_rows = slice(a * rows_out, (a + 1) * rows_out)
        if d == 1:
            out_ref[0, 0, out_rows, cols] = value.astype(BF16)
            return
        stage_ref[slot] = value
        step = DESTRIDE_STEP
        if d == step:
            for r in range(d):
                picked = stage_ref[slot, pl.ds(r, rows_out, stride=step), :]
                out_ref[0, r, out_rows, cols] = picked.astype(BF16)
            return
        part = sub // step
        for r_lo in range(step):
            mid_ref[slot, r_lo * part:(r_lo + 1) * part, :] = stage_ref[slot, pl.ds(r_lo, part, stride=step), :]
        for r_lo in range(step):
            for r_hi in range(step):
                r = r_lo + step * r_hi
                picked = mid_ref[slot, pl.ds(r_lo * part + r_hi, rows_out, stride=step), :]
                out_ref[0, r, out_rows, cols] = picked.astype(BF16)

    def pool_mix(u, row0):
        ext = jnp.concatenate([carry_ref[...], u], axis=0)
        carry_ref[...] = u[sub - MAX_POOL_WINDOW:, :]
        head = MAX_POOL_WINDOW
        head_row = lax.broadcasted_iota(jnp.int32, (head, LANES), 0) + row0 + 1
        mixed = []
        for half in range(POOL_WIDTH // LANES):
            w_a, w_b = POOL_WINDOWS[2 * half], POOL_WINDOWS[2 * half + 1]
            acc = ext[:, half * LANES:(half + 1) * LANES]
            width = 1
            while width < w_a:
                acc = acc + pltpu.roll(acc, width, 0)
                width *= 2
            sum_a = acc
            while width < w_b:
                acc = acc + pltpu.roll(acc, width, 0)
                width *= 2
            win = jnp.where(low_group, sum_a, acc)[MAX_POOL_WINDOW:, :]
            window = jnp.where(low_group_head, w_a, w_b)
            count = jnp.minimum(head_row, window).astype(F32)
            inv_window = jnp.where(low_group_row, 1.0 / w_a, 1.0 / w_b)
            mean = jnp.concatenate([win[:head] / count, win[head:] * inv_window], axis=0)
            mixed.append(mean - u[:, half * LANES:(half + 1) * LANES])
        mixed = jnp.concatenate(mixed, axis=1).astype(BF16)
        y = jnp.dot(mixed, wpool_ref[...], preferred_element_type=F32) + bpool_ref[...]
        return (y * pscale_ref[...]).astype(BF16)

    def prologue_piece(k, dst, xsrc_ref, row0, psrc_ref, prow0, gain, shift):
        rows = slice(k * PIECE, (k + 1) * PIECE)
        x = xsrc_ref[0, row0 + k * PIECE:row0 + (k + 1) * PIECE, :]
        h_ref[dst, rows, :] = (x * _rms_scale(x) * gain + shift).astype(BF16)
        if k < sub // LANES:
            pos = psrc_ref[0, 0, prow0 + k:prow0 + k + 1, :].astype(F32)
            cos, sin = _rope_tables(pos, freq_ref[...], expand_ref[...])
            for i, table in enumerate((cos, sin, cos * q_scale, sin * q_scale)):
                tab_ref[dst, i, k * LANES:(k + 1) * LANES, :] = table

    n_sub = tm // sub
    n_pieces = sub // PIECE
    shift = mod_ref[0, 0:1, :]
    gain = g_ref[...] * (1.0 + mod_ref[0, 1:2, :])
    shift_next = modn_ref[0, 0:1, :]
    gain_next = g_ref[...] * (1.0 + modn_ref[0, 1:2, :])

    @pl.when((pl.program_id(0) == 0) & (s == 0))
    def _():
        for k in range(n_pieces):
            prologue_piece(k, 0, x_ref, 0, pos_ref, 0, gain, shift)

    chunks = [(g, kind) for g in range(len(DILATIONS)) for kind in range(3)]
    chunks.remove((0, 2))
    chunks.insert(len(chunks) // 2, (None, None))
    chunks.append((0, 2))
    slots = {}
    for g, d in enumerate(DILATIONS):
        if d > 1:
            for kind in range(3):
                for half in range(HALVES):
                    slots[g, kind, half] = len(slots)

    for a in range(n_sub):
        rows = slice(a * sub, (a + 1) * sub)
        h = h_ref[a % 2]
        cos, sin, cos_q, sin_q = (tab_ref[a % 2, i] for i in range(4))
        for ci, (g, kind) in enumerate(chunks):
            if g is None:
                raw_ref[ci % 2] = jnp.dot(h, w_ref[:, :POOL_WIDTH], preferred_element_type=F32)
                pool_ref[0, rows, :] = pool_mix(raw_ref[ci % 2], s * tm + a * sub)
            else:
                lo = POOL_WIDTH + kind * ATTN_WIDTH + g * GROUP_WIDTH
                raw_ref[ci % 2] = jnp.dot(h, w_ref[:, lo:lo + GROUP_WIDTH], preferred_element_type=F32)
                for half in range(HALVES):
                    value = raw_ref[ci % 2, :, half * LANES:(half + 1) * LANES]
                    if kind == 0:
                        value = rope(value, cos_q, sin_q)
                    elif kind == 1:
                        value = rope(value, cos, sin)
                    emit(out_refs[g][kind], value, DILATIONS[g], half, a, slots.get((g, kind, half)))
            if ci < n_pieces:
                if a + 1 < n_sub:
                    prologue_piece(ci, (a + 1) % 2, x_ref, (a + 1) * sub, pos_ref, (a + 1) * (sub // LANES),
                                   gain, shift)
                else:
                    prologue_piece(ci, (a + 1) % 2, xn_ref, 0, posn_ref, 0, gain_next, shift_next)


def _inproj_call(x, mod3, g_pre, w_in, positions, wpool_blk, bpool, pscale):
    batch, seq, _ = x.shape
    tm = INPROJ_TILE
    row = lambda b, s: (b, s, 0)
    const = lambda b, s: (0, 0)
    resident = dict(pipeline_mode=pl.Buffered(1))
    inv_freq = ROPE_THETA ** (-jnp.arange(0, ROT_DIM, 2, dtype=F32) / ROT_DIM)
    pos4 = positions.reshape(batch, seq // tm, tm // LANES, LANES)
    expand = jnp.asarray(_rope_expansion(), BF16)
    steps = seq // tm

    def next_step(b, s):
        wrap = (s + 1) // steps
        return jnp.minimum(b + wrap, batch - 1), (s + 1) % steps

    out_specs = [pl.BlockSpec((1, tm, GROUP_WIDTH), row)]
    out_shape = [jax.ShapeDtypeStruct((batch, seq, GROUP_WIDTH), BF16)]
    n_stage = 0
    for d in DILATIONS:
        out_specs += [pl.BlockSpec((1, d, tm // d, GROUP_WIDTH), lambda b, s: (b, 0, s, 0))] * 3
        out_shape += [jax.ShapeDtypeStruct((batch, d, seq // d, GROUP_WIDTH), BF16)] * 3
        n_stage += 3 * HALVES if d > 1 else 0
    return pl.pallas_call(
        _inproj_kernel,
        grid=(batch, seq // tm),
        in_specs=[pl.BlockSpec((1, tm, D_MODEL), row),
                  pl.BlockSpec((1, INPROJ_SUB, D_MODEL),
                               lambda b, s: (next_step(b, s)[0], next_step(b, s)[1] * (tm // INPROJ_SUB), 0)),
                  pl.BlockSpec((1, N_MOD, D_MODEL), lambda b, s: (b, 0, 0)),
                  pl.BlockSpec((1, N_MOD, D_MODEL), lambda b, s: (next_step(b, s)[0], 0, 0)),
                  pl.BlockSpec((1, D_MODEL), const),
                  pl.BlockSpec((D_MODEL, IN_PROJ_WIDTH), const, **resident),
                  pl.BlockSpec((1, 1, tm // LANES, LANES), lambda b, s: (b, s, 0, 0)),
                  pl.BlockSpec((1, 1, tm // LANES, LANES), lambda b, s: (*next_step(b, s), 0, 0)),
                  pl.BlockSpec((ROT_HALF, 1), const),
                  pl.BlockSpec((LANES, 2 * LANES), const),
                  pl.BlockSpec((POOL_WIDTH, POOL_WIDTH), const),
                  pl.BlockSpec((1, POOL_WIDTH), const),
                  pl.BlockSpec((1, POOL_WIDTH), const)],
        out_specs=out_specs,
        out_shape=out_shape,
        scratch_shapes=[pltpu.VMEM((MAX_POOL_WINDOW, POOL_WIDTH), F32),
                        pltpu.VMEM((n_stage, INPROJ_SUB, LANES), F32),
                        pltpu.VMEM((n_stage, INPROJ_SUB, LANES), F32),
                        pltpu.VMEM((2, INPROJ_SUB, D_MODEL), BF16),
                        pltpu.VMEM((2, 4, INPROJ_SUB, LANES), F32),
                        pltpu.VMEM((2, INPROJ_SUB, GROUP_WIDTH), F32)],
        compiler_params=_params(2),
        name="inproj",
    )(x, x, mod3, mod3, g_pre, w_in, pos4, pos4, inv_freq.reshape(ROT_HALF, 1), expand, wpool_blk, bpool,
      pscale)


def _attend(q, keys, vals, bias, head_masks):
    q_heads = jnp.concatenate([jnp.where(m, q, jnp.zeros_like(q)) for m in head_masks], axis=0)
    scores = lax.dot_general(q_heads, keys, (((1,), (1,)), ((), ())), preferred_element_type=F32)
    probs, lses = [], []
    for h in range(HEADS_PER_GROUP):
        s = scores[h * BLOCK:(h + 1) * BLOCK] + bias
        m_row = jnp.max(s, axis=-1, keepdims=True)
        p = jnp.exp(s - m_row)
        den = jnp.sum(p, axis=-1, keepdims=True)
        probs.append((p * (1.0 / den)).astype(BF16))
        lses.append(m_row + jnp.log(den))
    pv = jnp.dot(jnp.concatenate(probs, axis=0), vals, preferred_element_type=F32)
    out = pv[(HEADS_PER_GROUP - 1) * BLOCK:]
    lse_out = jnp.broadcast_to(lses[-1], (BLOCK, GROUP_WIDTH))
    for h in range(HEADS_PER_GROUP - 2, -1, -1):
        out = jnp.where(head_masks[h], pv[h * BLOCK:(h + 1) * BLOCK], out)
        lse_out = jnp.where(head_masks[h], lses[h], lse_out)
    return out, lse_out


def _attn_kernel(q0_ref, kp0_ref, kc0_ref, vp0_ref, vc0_ref,
                 q1_ref, kp1_ref, kc1_ref, vp1_ref, vc1_ref,
                 q2_ref, kp2_ref, kc2_ref, vp2_ref, vc2_ref, wout_ref, wup_ref, wdown_ref,
                 o_ref, wout_bf_ref, wup_bf_ref, wdown_bf_ref, ost_ref, lst_ref):
    for src_ref, dst_ref in ((wout_ref, wout_bf_ref), (wup_ref, wup_bf_ref), (wdown_ref, wdown_bf_ref)):
        dst_ref[...] = src_ref[...].astype(BF16)
    first_chunk = pl.program_id(1) == 0
    i = lax.broadcasted_iota(jnp.int32, (BLOCK, 2 * BLOCK), 0)
    j = lax.broadcasted_iota(jnp.int32, (BLOCK, 2 * BLOCK), 1)
    valid = (j >= i) & (j <= i + SPAN)
    bias = jnp.where(valid, 0.0, MASK_VALUE)
    has_prev = jnp.logical_not(first_chunk)
    bias_first = jnp.where(valid & ((j >= BLOCK) | has_prev), 0.0, MASK_VALUE)
    lane = lax.broadcasted_iota(jnp.int32, (BLOCK, GROUP_WIDTH), 1)
    head_masks = [(lane // HEAD_DIM) == h for h in range(HEADS_PER_GROUP)]
    groups = ((q0_ref, kp0_ref, kc0_ref, vp0_ref, vc0_ref),
              (q1_ref, kp1_ref, kc1_ref, vp1_ref, vc1_ref),
              (q2_ref, kp2_ref, kc2_ref, vp2_ref, vc2_ref))

    def first_tile(g, r):
        q_ref, kp_ref, kc_ref, vp_ref, vc_ref = groups[g]
        keys = jnp.concatenate([kp_ref[0, r], kc_ref[0, r, :BLOCK, :]], axis=0)
        vals = jnp.concatenate([vp_ref[0, r], vc_ref[0, r, :BLOCK, :]], axis=0)
        return _attend(q_ref[0, r, :BLOCK, :], keys, vals, bias_first, head_masks)

    def later_tile(g, r, n):
        q_ref, _, kc_ref, _, vc_ref = groups[g]
        start = (n - 1) * BLOCK
        both = pl.ds(start, 2 * BLOCK)
        return _attend(q_ref[0, r, pl.ds(start + BLOCK, BLOCK), :], kc_ref[0, r, both, :],
                       vc_ref[0, r, both, :], bias, head_masks)

    def token_rows(g, r, n):
        d = DILATIONS[g]
        start = n * (d * BLOCK) + r
        return pl.ds(start, BLOCK, stride=d) if d > 1 else pl.ds(start, BLOCK)

    def merged(rows, out, lse, half):
        cols = slice(half * LANES, (half + 1) * LANES)
        o_a, l_a, o_b, l_b = ost_ref[half, rows, :], lst_ref[half, rows, :], out[:, cols], lse[:, cols]
        top = jnp.maximum(l_a, l_b)
        w_a = jnp.exp(l_a - top)
        w_b = jnp.exp(l_b - top)
        total = w_a + w_b
        return (w_a * o_a + w_b * o_b) * (1.0 / total), top + jnp.log(total)

    def start_state(rows, out, lse):
        for half in range(HALVES):
            cols = slice(half * LANES, (half + 1) * LANES)
            ost_ref[half, rows, :] = out[:, cols]
            lst_ref[half, rows, :] = lse[:, cols]

    def fold_state(rows, out, lse):
        for half in range(HALVES):
            ost_ref[half, rows, :], lst_ref[half, rows, :] = merged(rows, out, lse, half)

    def finish(rows, out, lse):
        for half in range(HALVES):
            o_ref[0, rows, half * LANES:(half + 1) * LANES] = merged(rows, out, lse, half)[0].astype(BF16)

    blocks = [ATTN_CHUNK // (d * BLOCK) for d in DILATIONS]
    for r in range(DILATIONS[2]):
        start_state(token_rows(2, r, 0), *first_tile(2, r))
    for r in range(DILATIONS[1]):
        fold_state(token_rows(1, r, 0), *first_tile(1, r))
        for n in range(1, blocks[1]):
            fold_state(token_rows(1, r, n), *later_tile(1, r, n))
    finish(token_rows(0, 0, 0), *first_tile(0, 0))
    for n in range(1, blocks[0]):
        finish(token_rows(0, 0, n), *later_tile(0, 0, n))


def _attn_call(qkv, weights):
    batch = qkv[0][0].shape[0]
    seq = qkv[0][0].shape[2]
    chunks = seq // ATTN_CHUNK
    n_steps = batch * chunks
    in_specs, args = [], []
    for (q, k, v), d in zip(qkv, DILATIONS):
        rows = ATTN_CHUNK // d
        cur = pl.BlockSpec((1, d, rows, GROUP_WIDTH), lambda b, c: (b, 0, c, 0))
        blocks_per_chunk = rows // BLOCK
        prev = pl.BlockSpec((1, d, BLOCK, GROUP_WIDTH),
                            lambda b, c, n=blocks_per_chunk: (b, 0, jnp.maximum(c * n - 1, 0), 0))
        in_specs += [cur, prev, cur, prev, cur]
        args += [q, k, k, v, v]
    weight_specs = [pl.BlockSpec((w.shape[0] // n_steps, w.shape[1]), lambda b, c: (b * chunks + c, 0))
                    for w in weights]
    return pl.pallas_call(
        _attn_kernel,
        grid=(batch, chunks),
        in_specs=in_specs + weight_specs,
        out_specs=[pl.BlockSpec((1, ATTN_CHUNK, GROUP_WIDTH), lambda b, c: (b, c, 0))] + weight_specs,
        out_shape=[jax.ShapeDtypeStruct((batch, seq, GROUP_WIDTH), BF16)]
                  + [jax.ShapeDtypeStruct(w.shape, BF16) for w in weights],
        scratch_shapes=[pltpu.VMEM((HALVES, ATTN_CHUNK, LANES), F32),
                        pltpu.VMEM((HALVES, ATTN_CHUNK, LANES), F32)],
        compiler_params=_params(2),
        name="attn",
    )(*args, *weights)


def _ffn_kernel(x_ref, pool_ref, attn_ref, modc_ref, moda_ref,
                gpostm_ref, gpref_ref, gpostf_ref, wout_ref, wup_ref, cw_ref, cb_ref, wdown_ref,
                out_ref, act_ref, carry_ref, x1a_ref, x1b_ref, h2a_ref, h2b_ref, ya_ref, yb_ref, raw_ref,
                *, steps_per_seq):
    j = pl.program_id(0)
    n_tiles = pl.num_programs(0) - 2
    tm = out_ref.shape[1]
    halo = carry_ref.shape[0]
    n_pieces = tm // FFN_PIECE

    @pl.when((j - 1) % steps_per_seq == 0)
    def _():
        carry_ref[...] = jnp.zeros_like(carry_ref)

    def prologue_piece(k, x1_dst, h2_dst):
        rows = slice(k * FFN_PIECE, (k + 1) * FFN_PIECE)
        mixed = jnp.concatenate([pool_ref[0, rows, :], attn_ref[0, rows, :]], axis=-1)
        y = jnp.dot(mixed, wout_ref[...], preferred_element_type=F32)
        x1 = x_ref[0, rows, :] + modc_ref[0, 2:3, :] * (y * _rms_scale(y) * gpostm_ref[...])
        x1_dst[rows, :] = x1
        h2 = x1 * _rms_scale(x1) * gpref_ref[...] * (1.0 + modc_ref[0, 4:5, :]) + modc_ref[0, 3:4, :]
        h2_dst[rows, :] = h2.astype(BF16)

    def finish_piece(k, x1_src, y_src):
        rows = slice(k * FFN_PIECE, (k + 1) * FFN_PIECE)
        y = y_src[rows, :]
        out_ref[0, rows, :] = x1_src[rows, :] + moda_ref[0, 5:6, :] * (y * _rms_scale(y) * gpostf_ref[...])

    def step(x1_other, h2_cur, h2_other, y_cur, y_other):
        h2 = h2_cur[...]
        n_chunks = D_FF // FF_CHUNK
        for c in range(n_chunks):
            cols = slice(c * FF_CHUNK, (c + 1) * FF_CHUNK)
            vcols = slice(D_FF + c * FF_CHUNK, D_FF + (c + 1) * FF_CHUNK)
            raw_ref[c % 2, 0] = jnp.dot(h2, wup_ref[:, cols], preferred_element_type=F32)
            raw_ref[c % 2, 1] = jnp.dot(h2, wup_ref[:, vcols], preferred_element_type=F32)
            gate, val = raw_ref[c % 2, 0], raw_ref[c % 2, 1]
            ext = jnp.concatenate([carry_ref[:, cols], gate], axis=0)
            carry_ref[:, cols] = gate[tm - halo:, :]
            back1 = pltpu.roll(ext, 1, 0)[halo:, :]
            back2 = pltpu.roll(ext, 2, 0)[halo:, :]
            conv = (back2 * cw_ref[0:1, cols] + back1 * cw_ref[1:2, cols] + gate * cw_ref[2:3, cols]
                    + cb_ref[:, cols])
            act_ref[:, cols] = (jax.nn.gelu(conv, approximate=True) * val).astype(BF16)
            if c % 2 == 1 and c // 2 < n_pieces:
                finish_piece(c // 2, x1_other, y_other)
        act = act_ref[...]
        for k in range(D_MODEL // DOWN_CHUNK):
            if k < n_pieces:
                prologue_piece(k, x1_other, h2_other)
            cols = slice(k * DOWN_CHUNK, (k + 1) * DOWN_CHUNK)
            y_cur[:, cols] = jnp.dot(act, wdown_ref[:, cols], preferred_element_type=F32)

    @pl.when(j == 0)
    def _():
        x1b_ref[...] = jnp.zeros_like(x1b_ref)
        yb_ref[...] = jnp.zeros_like(yb_ref)
        for k in range(n_pieces):
            prologue_piece(k, x1a_ref, h2a_ref)

    @pl.when((j >= 1) & (j <= n_tiles) & (j % 2 == 1))
    def _():
        step(x1b_ref, h2a_ref, h2b_ref, ya_ref, yb_ref)

    @pl.when((j >= 1) & (j <= n_tiles) & (j % 2 == 0))
    def _():
        step(x1a_ref, h2b_ref, h2a_ref, yb_ref, ya_ref)

    @pl.when(j == n_tiles + 1)
    def _():
        for k in range(n_pieces):
            finish_piece(k, x1b_ref, yb_ref)


def _ffn_call(x, pool, attn, mod3, g_post_mix, g_pre_ffn, g_post_ffn, w_out, w_up, conv_w, conv_b, w_down):
    batch, seq, _ = x.shape
    tm = ROW_TILE
    steps = seq // tm
    n_tiles = batch * steps
    assert n_tiles % 2 == 0
    const = lambda j: (0, 0)

    def tile(t):
        t = jnp.clip(t, 0, n_tiles - 1)
        return t // steps, t % steps

    resident = dict(pipeline_mode=pl.Buffered(1))
    vec = pl.BlockSpec((1, D_MODEL), const)
    return pl.pallas_call(
        functools.partial(_ffn_kernel, steps_per_seq=steps),
        grid=(n_tiles + 2,),
        in_specs=[pl.BlockSpec((1, tm, D_MODEL), lambda j: (*tile(j), 0)),
                  pl.BlockSpec((1, tm, GROUP_WIDTH), lambda j: (*tile(j), 0)),
                  pl.BlockSpec((1, tm, GROUP_WIDTH), lambda j: (*tile(j), 0)),
                  pl.BlockSpec((1, N_MOD, D_MODEL), lambda j: (tile(j)[0], 0, 0)),
                  pl.BlockSpec((1, N_MOD, D_MODEL), lambda j: (tile(j - 2)[0], 0, 0)),
                  vec, vec, vec,
                  pl.BlockSpec((POOL_WIDTH + GROUP_WIDTH, D_MODEL), const, **resident),
                  pl.BlockSpec((D_MODEL, 2 * D_FF), const, **resident),
                  pl.BlockSpec((3, D_FF), const),
                  pl.BlockSpec((1, D_FF), const),
                  pl.BlockSpec((D_FF, D_MODEL), const, **resident)],
        out_specs=pl.BlockSpec((1, tm, D_MODEL), lambda j: (*tile(j - 2), 0)),
        out_shape=jax.ShapeDtypeStruct((batch, seq, D_MODEL), F32),
        scratch_shapes=[pltpu.VMEM((tm, D_FF), BF16),
                        pltpu.VMEM((SUBLANES, D_FF), F32),
                        pltpu.VMEM((tm, D_MODEL), F32),
                        pltpu.VMEM((tm, D_MODEL), F32),
                        pltpu.VMEM((tm, D_MODEL), BF16),
                        pltpu.VMEM((tm, D_MODEL), BF16),
                        pltpu.VMEM((tm, D_MODEL), F32),
                        pltpu.VMEM((tm, D_MODEL), F32),
                        pltpu.VMEM((2, 2, tm, FF_CHUNK), F32)],
        compiler_params=_params(1),
        name="ffn",
    )(x, pool, attn, mod3, mod3, g_post_mix, g_pre_ffn, g_post_ffn, w_out, w_up, conv_w, conv_b, w_down)


def kernel(x, c, positions, w_ada, b_ada, g_pre_mix, g_post_mix, g_pre_ffn, g_post_ffn,
           w_in, w_pool, b_pool, pool_scale, w_out, w_up, conv_w, conv_b, w_down):
    depth = w_ada.shape[0]
    batch = x.shape[0]
    for l in range(depth):
        mod3 = _mod_call(c, w_ada[l], b_ada[l]).reshape(batch, N_MOD, D_MODEL)
        wpool_blk = jax.scipy.linalg.block_diag(*[w_pool[l, g] for g in range(len(POOL_WINDOWS))])
        pool, *qkv = _inproj_call(
            x, mod3, g_pre_mix[l].reshape(1, D_MODEL), w_in[l].astype(BF16), positions,
            wpool_blk.astype(BF16), b_pool[l].reshape(1, POOL_WIDTH), pool_scale[l].reshape(1, POOL_WIDTH))
        attn, w_out_bf, w_up_bf, w_down_bf = _attn_call([qkv[0:3], qkv[3:6], qkv[6:9]],
                                                         [w_out[l], w_up[l], w_down[l]])
        x = _ffn_call(x, pool, attn, mod3, g_post_mix[l].reshape(1, D_MODEL), g_pre_ffn[l].reshape(1, D_MODEL),
                      g_post_ffn[l].reshape(1, D_MODEL), w_out_bf, w_up_bf, conv_w[l],
                      conv_b[l].reshape(1, D_FF), w_down_bf)
    return x
```

```python
import numpy as np

import jax
import jax.numpy as jnp
from jax import lax
from jax.experimental import pallas as pl
from jax.experimental.pallas import tpu as pltpu

F32 = jnp.float32
BF16 = jnp.bfloat16

D_MODEL = 1024
HEAD_DIM = 64
POOL_WIDTH = 256
POOL_WINDOWS = (2, 4, 8, 16)
POOL_GROUP_DIM = 64
MAX_POOL_WINDOW = 16
ATTN_WIDTH = 768
GROUP_WIDTH = 256
HEADS_PER_GROUP = 4
DILATIONS = (1, 4, 16)
DESTRIDE_STEP = 4
SPAN = 128
BLOCK = 128
IN_PROJ_WIDTH = 2560
ROT_DIM = 16
ROT_HALF = 8
ROPE_THETA = 500000.0
D_FF = 2816
NORM_EPS = 1e-6
N_MOD = 6
MASK_VALUE = -1e30

LANES = 128
SUBLANES = 8
VMEM_LIMIT_BYTES = 56 * 1024 * 1024

ROW_TILE = 512
FFN_PIECE = 128
DOWN_CHUNK = 256
INPROJ_TILE = 1024
INPROJ_SUB = 512
PIECE = 64
FF_CHUNK = 256
ATTN_CHUNK = BLOCK * DILATIONS[-1]
HALVES = GROUP_WIDTH // LANES
N_SPLIT = 3


def _params(n_axes, vmem=VMEM_LIMIT_BYTES):
    return pltpu.CompilerParams(dimension_semantics=("arbitrary",) * n_axes, vmem_limit_bytes=vmem)


def _rms_scale(v):
    return lax.rsqrt(jnp.mean(v * v, axis=-1, keepdims=True) + NORM_EPS)


def _mod_kernel(c_ref, w_ref, b_ref, win_ref, o_ref, win_bf_ref):
    c = c_ref[...]
    c_act = c * jax.nn.sigmoid(c)
    o_ref[...] = jnp.dot(c_act.astype(BF16), w_ref[...].astype(BF16),
                         preferred_element_type=F32) + b_ref[...]
    win_bf_ref[...] = win_ref[...].astype(BF16)


def _mod_call(c, w_ada, b_ada, w_in):
    batch = c.shape[0]
    n = w_ada.shape[1]
    tn = 768
    steps = n // tn
    slab = pl.BlockSpec((w_in.shape[0] // steps, w_in.shape[1]), lambda j: (j, 0))
    return pl.pallas_call(
        _mod_kernel,
        grid=(steps,),
        in_specs=[pl.BlockSpec((batch, D_MODEL), lambda j: (0, 0)),
                  pl.BlockSpec((D_MODEL, tn), lambda j: (0, j)),
                  pl.BlockSpec((1, tn), lambda j: (0, j)),
                  slab],
        out_specs=[pl.BlockSpec((batch, tn), lambda j: (0, j)), slab],
        out_shape=[jax.ShapeDtypeStruct((batch, n), F32), jax.ShapeDtypeStruct(w_in.shape, BF16)],
        compiler_params=_params(1),
        name="mod",
    )(c, w_ada, b_ada.reshape(1, n), w_in)


def _rope_expansion():
    e = np.zeros((LANES, 2 * LANES), np.float32)
    ones_row = 2 * N_SPLIT * ROT_HALF
    for lane in range(LANES):
        in_head = lane % HEAD_DIM
        if in_head >= ROT_DIM:
            e[ones_row, lane] = 1.0
            continue
        sign = -1.0 if in_head < ROT_HALF else 1.0
        for t in range(N_SPLIT):
            e[t * ROT_HALF + in_head % ROT_HALF, lane] = 1.0
            e[(N_SPLIT + t) * ROT_HALF + in_head % ROT_HALF, LANES + lane] = sign
    return e


def _rope_tables(pos_rows, freq_col, expand):
    cos_parts, sin_parts = [], []
    for j in range(pos_rows.shape[0]):
        ang = freq_col * pos_rows[j:j + 1, :]
        terms = []
        for table in (jnp.cos(ang), jnp.sin(ang)):
            rest = table
            for _ in range(N_SPLIT):
                part = rest.astype(BF16).astype(F32)
                terms.append(part)
                rest = rest - part
        terms.append(jnp.ones((SUBLANES, LANES), F32))
        pad = jnp.zeros((LANES - SUBLANES * len(terms), LANES), F32)
        stacked = jnp.concatenate(terms + [pad], axis=0)
        both = jnp.dot(stacked.T.astype(BF16), expand, preferred_element_type=F32)
        cos_parts.append(both[:, :LANES])
        sin_parts.append(both[:, LANES:])
    return jnp.concatenate(cos_parts, axis=0), jnp.concatenate(sin_parts, axis=0)


def _inproj_kernel(x_ref, xn_ref, mod_ref, modn_ref, g_ref, w_ref, pos_ref, posn_ref, freq_ref, expand_ref,
                   wpool_ref, bpool_ref, pscale_ref,
                   pool_ref, q0_ref, k0_ref, v0_ref, q1_ref, k1_ref, v1_ref, q2_ref, k2_ref, v2_ref,
                   carry_ref, stage_ref, mid_ref, h_ref, tab_ref, raw_ref):
    s = pl.program_id(1)
    tm = x_ref.shape[1]
    sub = INPROJ_SUB
    q_scale = HEAD_DIM ** -0.5
    first_half = (lax.broadcasted_iota(jnp.int32, (sub, LANES), 1) % HEAD_DIM) < ROT_HALF
    low_group = lax.broadcasted_iota(jnp.int32, (sub + MAX_POOL_WINDOW, LANES), 1) < POOL_GROUP_DIM
    low_group_head = lax.broadcasted_iota(jnp.int32, (MAX_POOL_WINDOW, LANES), 1) < POOL_GROUP_DIM
    low_group_row = lax.broadcasted_iota(jnp.int32, (1, LANES), 1) < POOL_GROUP_DIM
    out_refs = ((q0_ref, k0_ref, v0_ref), (q1_ref, k1_ref, v1_ref), (q2_ref, k2_ref, v2_ref))

    @pl.when(s == 0)
    def _():
        carry_ref[...] = jnp.zeros_like(carry_ref)

    def rope(t, cos, sin):
        partner = jnp.where(first_half, pltpu.roll(t, LANES - ROT_HALF, 1), pltpu.roll(t, ROT_HALF, 1))
        return t * cos + partner * sin

    def emit(out_ref, value, d, half, a, slot):
        cols = slice(half * LANES, (half + 1) * LANES)
        rows_out = sub // d
        out_rows = slice(a * rows_out, (a + 1) * rows_out)
        if d == 1:
            out_ref[0, 0, out_rows, cols] = value.astype(BF16)
            return
        stage_ref[slot] = value
        step = DESTRIDE_STEP
        if d == step:
            for r in range(d):
                picked = stage_ref[slot, pl.ds(r, rows_out, stride=step), :]
                out_ref[0, r, out_rows, cols] = picked.astype(BF16)
            return
        part = sub // step
        for r_lo in range(step):
            mid_ref[slot, r_lo * part:(r_lo + 1) * part, :] = stage_ref[slot, pl.ds(r_lo, part, stride=step), :]
        for r_lo in range(step):
            for r_hi in range(step):
                r = r_lo + step * r_hi
                picked = mid_ref[slot, pl.ds(r_lo * part + r_hi, rows_out, stride=step), :]
                out_ref[0, r, out_rows, cols] = picked.astype(BF16)

    def pool_mix(u, row0):
        ext = jnp.concatenate([carry_ref[...], u], axis=0)
        carry_ref[...] = u[sub - MAX_POOL_WINDOW:, :]
        head = MAX_POOL_WINDOW
        head_row = lax.broadcasted_iota(jnp.int32, (head, LANES), 0) + row0 + 1
        mixed = []
        for half in range(POOL_WIDTH // LANES):
            w_a, w_b = POOL_WINDOWS[2 * half], POOL_WINDOWS[2 * half + 1]
            acc = ext[:, half * LANES:(half + 1) * LANES]
            width = 1
            while width < w_a:
                acc = acc + pltpu.roll(acc, width, 0)
                width *= 2
            sum_a = acc
            while width < w_b:
                acc = acc + pltpu.roll(acc, width, 0)
                width *= 2
            win = jnp.where(low_group, sum_a, acc)[MAX_POOL_WINDOW:, :]
            window = jnp.where(low_group_head, w_a, w_b)
            count = jnp.minimum(head_row, window).astype(F32)
            inv_window = jnp.where(low_group_row, 1.0 / w_a, 1.0 / w_b)
            mean = jnp.concatenate([win[:head] / count, win[head:] * inv_window], axis=0)
            mixed.append(mean - u[:, half * LANES:(half + 1) * LANES])
        mixed = jnp.concatenate(mixed, axis=1).astype(BF16)
        y = jnp.dot(mixed, wpool_ref[...], preferred_element_type=F32) + bpool_ref[...]
        return (y * pscale_ref[...]).astype(BF16)

    def prologue_piece(k, dst, xsrc_ref, row0, psrc_ref, prow0, gain, shift):
        rows = slice(k * PIECE, (k + 1) * PIECE)
        x = xsrc_ref[0, row0 + k * PIECE:row0 + (k + 1) * PIECE, :]
        h_ref[dst, rows, :] = (x * _rms_scale(x) * gain + shift).astype(BF16)
        if k < sub // LANES:
            pos = psrc_ref[0, 0, prow0 + k:prow0 + k + 1, :].astype(F32)
            cos, sin = _rope_tables(pos, freq_ref[...], expand_ref[...])
            for i, table in enumerate((cos, sin, cos * q_scale, sin * q_scale)):
                tab_ref[dst, i, k * LANES:(k + 1) * LANES, :] = table

    n_sub = tm // sub
    n_pieces = sub // PIECE
    shift = mod_ref[0, 0:1, :]
    gain = g_ref[...] * (1.0 + mod_ref[0, 1:2, :])
    shift_next = modn_ref[0, 0:1, :]
    gain_next = g_ref[...] * (1.0 + modn_ref[0, 1:2, :])

    @pl.when((pl.program_id(0) == 0) & (s == 0))
    def _():
        for k in range(n_pieces):
            prologue_piece(k, 0, x_ref, 0, pos_ref, 0, gain, shift)

    chunks = [(g, kind) for g in range(len(DILATIONS)) for kind in range(3)]
    chunks.remove((0, 2))
    chunks.insert(len(chunks) // 2, (None, None))
    chunks.append((0, 2))
    slots = {}
    for g, d in enumerate(DILATIONS):
        if d > 1:
            for kind in range(3):
                for half in range(HALVES):
                    slots[g, kind, half] = len(slots)

    for a in range(n_sub):
        rows = slice(a * sub, (a + 1) * sub)
        h = h_ref[a % 2]
        cos, sin, cos_q, sin_q = (tab_ref[a % 2, i] for i in range(4))
        for ci, (g, kind) in enumerate(chunks):
            if g is None:
                raw_ref[ci % 2] = jnp.dot(h, w_ref[:, :POOL_WIDTH], preferred_element_type=F32)
                pool_ref[0, rows, :] = pool_mix(raw_ref[ci % 2], s * tm + a * sub)
            else:
                lo = POOL_WIDTH + kind * ATTN_WIDTH + g * GROUP_WIDTH
                raw_ref[ci % 2] = jnp.dot(h, w_ref[:, lo:lo + GROUP_WIDTH], preferred_element_type=F32)
                for half in range(HALVES):
                    value = raw_ref[ci % 2, :, half * LANES:(half + 1) * LANES]
                    if kind == 0:
                        value = rope(value, cos_q, sin_q)
                    elif kind == 1:
                        value = rope(value, cos, sin)
                    emit(out_refs[g][kind], value, DILATIONS[g], half, a, slots.get((g, kind, half)))
            if ci < n_pieces:
                if a + 1 < n_sub:
                    prologue_piece(ci, (a + 1) % 2, x_ref, (a + 1) * sub, pos_ref, (a + 1) * (sub // LANES),
                                   gain, shift)
                else:
                    prologue_piece(ci, (a + 1) % 2, xn_ref, 0, posn_ref, 0, gain_next, shift_next)


def _inproj_call(x, mod3, g_pre, w_in, positions, wpool_blk, bpool, pscale):
    batch, seq, _ = x.shape
    tm = INPROJ_TILE
    row = lambda b, s: (b, s, 0)
    const = lambda b, s: (0, 0)
    resident = dict(pipeline_mode=pl.Buffered(1))
    inv_freq = ROPE_THETA ** (-jnp.arange(0, ROT_DIM, 2, dtype=F32) / ROT_DIM)
    pos4 = positions.reshape(batch, seq // tm, tm // LANES, LANES)
    expand = jnp.asarray(_rope_expansion(), BF16)
    steps = seq // tm

    def next_step(b, s):
        wrap = (s + 1) // steps
        return jnp.minimum(b + wrap, batch - 1), (s + 1) % steps

    out_specs = [pl.BlockSpec((1, tm, GROUP_WIDTH), row)]
    out_shape = [jax.ShapeDtypeStruct((batch, seq, GROUP_WIDTH), BF16)]
    n_stage = 0
    for d in DILATIONS:
        out_specs += [pl.BlockSpec((1, d, tm // d, GROUP_WIDTH), lambda b, s: (b, 0, s, 0))] * 3
        out_shape += [jax.ShapeDtypeStruct((batch, d, seq // d, GROUP_WIDTH), BF16)] * 3
        n_stage += 3 * HALVES if d > 1 else 0
    return pl.pallas_call(
        _inproj_kernel,
        grid=(batch, seq // tm),
        in_specs=[pl.BlockSpec((1, tm, D_MODEL), row),
                  pl.BlockSpec((1, INPROJ_SUB, D_MODEL),
                               lambda b, s: (next_step(b, s)[0], next_step(b, s)[1] * (tm // INPROJ_SUB), 0)),
                  pl.BlockSpec((1, N_MOD, D_MODEL), lambda b, s: (b, 0, 0)),
                  pl.BlockSpec((1, N_MOD, D_MODEL), lambda b, s: (next_step(b, s)[0], 0, 0)),
                  pl.BlockSpec((1, D_MODEL), const),
                  pl.BlockSpec((D_MODEL, IN_PROJ_WIDTH), const, **resident),
                  pl.BlockSpec((1, 1, tm // LANES, LANES), lambda b, s: (b, s, 0, 0)),
                  pl.BlockSpec((1, 1, tm // LANES, LANES), lambda b, s: (*next_step(b, s), 0, 0)),
                  pl.BlockSpec((ROT_HALF, 1), const),
                  pl.BlockSpec((LANES, 2 * LANES), const),
                  pl.BlockSpec((POOL_WIDTH, POOL_WIDTH), const),
                  pl.BlockSpec((1, POOL_WIDTH), const),
                  pl.BlockSpec((1, POOL_WIDTH), const)],
        out_specs=out_specs,
        out_shape=out_shape,
        scratch_shapes=[pltpu.VMEM((MAX_POOL_WINDOW, POOL_WIDTH), F32),
                        pltpu.VMEM((n_stage, INPROJ_SUB, LANES), F32),
                        pltpu.VMEM((n_stage, INPROJ_SUB, LANES), F32),
                        pltpu.VMEM((2, INPROJ_SUB, D_MODEL), BF16),
                        pltpu.VMEM((2, 4, INPROJ_SUB, LANES), F32),
                        pltpu.VMEM((2, INPROJ_SUB, GROUP_WIDTH), F32)],
        compiler_params=_params(2),
        name="inproj",
    )(x, x, mod3, mod3, g_pre, w_in, pos4, pos4, inv_freq.reshape(ROT_HALF, 1), expand, wpool_blk, bpool,
      pscale)


def _attend(q, keys, vals, bias, head_masks):
    q_heads = jnp.concatenate([jnp.where(m, q, jnp.zeros_like(q)) for m in head_masks], axis=0)
    scores = lax.dot_general(q_heads, keys, (((1,), (1,)), ((), ())), preferred_element_type=F32)
    probs, lses = [], []
    for h in range(HEADS_PER_GROUP):
        s = scores[h * BLOCK:(h + 1) * BLOCK] + bias
        m_row = jnp.max(s, axis=-1, keepdims=True)
        p = jnp.exp(s - m_row)
        den = jnp.sum(p, axis=-1, keepdims=True)
        probs.append((p * (1.0 / den)).astype(BF16))
        lses.append(m_row + jnp.log(den))
    pv = jnp.dot(jnp.concatenate(probs, axis=0), vals, preferred_element_type=F32)
    out = pv[(HEADS_PER_GROUP - 1) * BLOCK:]
    lse_out = jnp.broadcast_to(lses[-1], (BLOCK, GROUP_WIDTH))
    for h in range(HEADS_PER_GROUP - 2, -1, -1):
        out = jnp.where(head_masks[h], pv[h * BLOCK:(h + 1) * BLOCK], out)
        lse_out = jnp.where(head_masks[h], lses[h], lse_out)
    return out, lse_out


def _attn_kernel(q0_ref, kp0_ref, kc0_ref, vp0_ref, vc0_ref,
                 q1_ref, kp1_ref, kc1_ref, vp1_ref, vc1_ref,
                 q2_ref, kp2_ref, kc2_ref, vp2_ref, vc2_ref, wout_ref, wup_ref, wdown_ref,
                 o_ref, wout_bf_ref, wup_bf_ref, wdown_bf_ref, ost_ref, lst_ref):
    for src_ref, dst_ref in ((wout_ref, wout_bf_ref), (wup_ref, wup_bf_ref), (wdown_ref, wdown_bf_ref)):
        dst_ref[...] = src_ref[...].astype(BF16)
    first_chunk = pl.program_id(1) == 0
    i = lax.broadcasted_iota(jnp.int32, (BLOCK, 2 * BLOCK), 0)
    j = lax.broadcasted_iota(jnp.int32, (BLOCK, 2 * BLOCK), 1)
    valid = (j >= i) & (j <= i + SPAN)
    bias = jnp.where(valid, 0.0, MASK_VALUE)
    has_prev = jnp.logical_not(first_chunk)
    bias_first = jnp.where(valid & ((j >= BLOCK) | has_prev), 0.0, MASK_VALUE)
    lane = lax.broadcasted_iota(jnp.int32, (BLOCK, GROUP_WIDTH), 1)
    head_masks = [(lane // HEAD_DIM) == h for h in range(HEADS_PER_GROUP)]
    groups = ((q0_ref, kp0_ref, kc0_ref, vp0_ref, vc0_ref),
              (q1_ref, kp1_ref, kc1_ref, vp1_ref, vc1_ref),
              (q2_ref, kp2_ref, kc2_ref, vp2_ref, vc2_ref))

    def first_tile(g, r):
        q_ref, kp_ref, kc_ref, vp_ref, vc_ref = groups[g]
        keys = jnp.concatenate([kp_ref[0, r], kc_ref[0, r, :BLOCK, :]], axis=0)
        vals = jnp.concatenate([vp_ref[0, r], vc_ref[0, r, :BLOCK, :]], axis=0)
        return _attend(q_ref[0, r, :BLOCK, :], keys, vals, bias_first, head_masks)

    def later_tile(g, r, n):
        q_ref, _, kc_ref, _, vc_ref = groups[g]
        start = (n - 1) * BLOCK
        both = pl.ds(start, 2 * BLOCK)
        return _attend(q_ref[0, r, pl.ds(start + BLOCK, BLOCK), :], kc_ref[0, r, both, :],
                       vc_ref[0, r, both, :], bias, head_masks)

    def token_rows(g, r, n):
        d = DILATIONS[g]
        start = n * (d * BLOCK) + r
        return pl.ds(start, BLOCK, stride=d) if d > 1 else pl.ds(start, BLOCK)

    def merged(rows, out, lse, half):
        cols = slice(half * LANES, (half + 1) * LANES)
        o_a, l_a, o_b, l_b = ost_ref[half, rows, :], lst_ref[half, rows, :], out[:, cols], lse[:, cols]
        top = jnp.maximum(l_a, l_b)
        w_a = jnp.exp(l_a - top)
        w_b = jnp.exp(l_b - top)
        total = w_a + w_b
        return (w_a * o_a + w_b * o_b) * (1.0 / total), top + jnp.log(total)

    def start_state(rows, out, lse):
        for half in range(HALVES):
            cols = slice(half * LANES, (half + 1) * LANES)
            ost_ref[half, rows, :] = out[:, cols]
            lst_ref[half, rows, :] = lse[:, cols]

    def fold_state(rows, out, lse):
        for half in range(HALVES):
            ost_ref[half, rows, :], lst_ref[half, rows, :] = merged(rows, out, lse, half)

    def finish(rows, out, lse):
        for half in range(HALVES):
            o_ref[0, rows, half * LANES:(half + 1) * LANES] = merged(rows, out, lse, half)[0].astype(BF16)

    blocks = [ATTN_CHUNK // (d * BLOCK) for d in DILATIONS]
    for r in range(DILATIONS[2]):
        start_state(token_rows(2, r, 0), *first_tile(2, r))
    for r in range(DILATIONS[1]):
        fold_state(token_rows(1, r, 0), *first_tile(1, r))
        for n in range(1, blocks[1]):
            fold_state(token_rows(1, r, n), *later_tile(1, r, n))
    finish(token_rows(0, 0, 0), *first_tile(0, 0))
    for n in range(1, blocks[0]):
        finish(token_rows(0, 0, n), *later_tile(0, 0, n))


def _attn_call(qkv, weights):
    batch = qkv[0][0].shape[0]
    seq = qkv[0][0].shape[2]
    chunks = seq // ATTN_CHUNK
    n_steps = batch * chunks
    in_specs, args = [], []
    for (q, k, v), d in zip(qkv, DILATIONS):
        rows = ATTN_CHUNK // d
        cur = pl.BlockSpec((1, d, rows, GROUP_WIDTH), lambda b, c: (b, 0, c, 0))
        blocks_per_chunk = rows // BLOCK
        prev = pl.BlockSpec((1, d, BLOCK, GROUP_WIDTH),
                            lambda b, c, n=blocks_per_chunk: (b, 0, jnp.maximum(c * n - 1, 0), 0))
        in_specs += [cur, prev, cur, prev, cur]
        args += [q, k, k, v, v]
    weight_specs = [pl.BlockSpec((w.shape[0] // n_steps, w.shape[1]), lambda b, c: (b * chunks + c, 0))
                    for w in weights]
    return pl.pallas_call(
        _attn_kernel,
        grid=(batch, chunks),
        in_specs=in_specs + weight_specs,
        out_specs=[pl.BlockSpec((1, ATTN_CHUNK, GROUP_WIDTH), lambda b, c: (b, c, 0))] + weight_specs,
        out_shape=[jax.ShapeDtypeStruct((batch, seq, GROUP_WIDTH), BF16)]
                  + [jax.ShapeDtypeStruct(w.shape, BF16) for w in weights],
        scratch_shapes=[pltpu.VMEM((HALVES, ATTN_CHUNK, LANES), F32),
                        pltpu.VMEM((HALVES, ATTN_CHUNK, LANES), F32)],
        compiler_params=_params(2),
        name="attn",
    )(*args, *weights)


def _ffn_kernel(x_ref, pool_ref, attn_ref, xn_ref, pooln_ref, attnn_ref, mod_ref, modn_ref,
                gpostm_ref, gpref_ref, gpostf_ref, wout_ref, wup_ref, cw_ref, cb_ref, wdown_ref,
                out_ref, act_ref, carry_ref, x1a_ref, x1b_ref, h2a_ref, h2b_ref, raw_ref):
    s = pl.program_id(1)
    tm = out_ref.shape[1]
    halo = carry_ref.shape[0]
    n_pieces = tm // FFN_PIECE

    @pl.when(s == 0)
    def _():
        carry_ref[...] = jnp.zeros_like(carry_ref)

    def prologue_piece(k, x1_dst, h2_dst, xs_ref, ps_ref, as_ref, ms_ref):
        rows = slice(k * FFN_PIECE, (k + 1) * FFN_PIECE)
        mixed = jnp.concatenate([ps_ref[0, rows, :], as_ref[0, rows, :]], axis=-1)
        y = jnp.dot(mixed, wout_ref[...], preferred_element_type=F32)
        x1 = xs_ref[0, rows, :] + ms_ref[0, 2:3, :] * (y * _rms_scale(y) * gpostm_ref[...])
        x1_dst[rows, :] = x1
        h2 = x1 * _rms_scale(x1) * gpref_ref[...] * (1.0 + ms_ref[0, 4:5, :]) + ms_ref[0, 3:4, :]
        h2_dst[rows, :] = h2.astype(BF16)

    def step(x1_cur, h2_cur, x1_next, h2_next):
        h2 = h2_cur[...]
        n_chunks = D_FF // FF_CHUNK
        for c in range(n_chunks):
            cols = slice(c * FF_CHUNK, (c + 1) * FF_CHUNK)
            vcols = slice(D_FF + c * FF_CHUNK, D_FF + (c + 1) * FF_CHUNK)
            raw_ref[c % 2, 0] = jnp.dot(h2, wup_ref[:, cols], preferred_element_type=F32)
            raw_ref[c % 2, 1] = jnp.dot(h2, wup_ref[:, vcols], preferred_element_type=F32)
            gate, val = raw_ref[c % 2, 0], raw_ref[c % 2, 1]
            ext = jnp.concatenate([carry_ref[:, cols], gate], axis=0)
            carry_ref[:, cols] = gate[tm - halo:, :]
            back1 = pltpu.roll(ext, 1, 0)[halo:, :]
            back2 = pltpu.roll(ext, 2, 0)[halo:, :]
            conv = (back2 * cw_ref[0:1, cols] + back1 * cw_ref[1:2, cols] + gate * cw_ref[2:3, cols]
                    + cb_ref[:, cols])
            act_ref[:, cols] = (jax.nn.gelu(conv, approximate=True) * val).astype(BF16)
        act = act_ref[...]
        y_parts = []
        for k in range(D_MODEL // DOWN_CHUNK):
            if k < n_pieces:
                prologue_piece(k, x1_next, h2_next, xn_ref, pooln_ref, attnn_ref, modn_ref)
            y_parts.append(jnp.dot(act, wdown_ref[:, k * DOWN_CHUNK:(k + 1) * DOWN_CHUNK],
                                   preferred_element_type=F32))
        y = jnp.concatenate(y_parts, axis=-1)
        out_ref[0] = x1_cur[...] + mod_ref[0, 5:6, :] * (y * _rms_scale(y) * gpostf_ref[...])

    @pl.when((pl.program_id(0) == 0) & (s == 0))
    def _():
        for k in range(n_pieces):
            prologue_piece(k, x1a_ref, h2a_ref, x_ref, pool_ref, attn_ref, mod_ref)

    @pl.when(s % 2 == 0)
    def _():
        step(x1a_ref, h2a_ref, x1b_ref, h2b_ref)

    @pl.when(s % 2 == 1)
    def _():
        step(x1b_ref, h2b_ref, x1a_ref, h2a_ref)


def _ffn_call(x, pool, attn, mod3, g_post_mix, g_pre_ffn, g_post_ffn, w_out, w_up, conv_w, conv_b, w_down):
    batch, seq, _ = x.shape
    tm = ROW_TILE
    steps = seq // tm
    assert steps % 2 == 0
    row = lambda b, s: (b, s, 0)
    const = lambda b, s: (0, 0)

    def nxt(b, s):
        return jnp.minimum(b + (s + 1) // steps, batch - 1), (s + 1) % steps, 0

    resident = dict(pipeline_mode=pl.Buffered(1))
    vec = pl.BlockSpec((1, D_MODEL), const)
    return pl.pallas_call(
        _ffn_kernel,
        grid=(batch, steps),
        in_specs=[pl.BlockSpec((1, tm, D_MODEL), row),
                  pl.BlockSpec((1, tm, GROUP_WIDTH), row),
                  pl.BlockSpec((1, tm, GROUP_WIDTH), row),
                  pl.BlockSpec((1, tm, D_MODEL), nxt),
                  pl.BlockSpec((1, tm, GROUP_WIDTH), nxt),
                  pl.BlockSpec((1, tm, GROUP_WIDTH), nxt),
                  pl.BlockSpec((1, N_MOD, D_MODEL), lambda b, s: (b, 0, 0)),
                  pl.BlockSpec((1, N_MOD, D_MODEL), lambda b, s: (nxt(b, s)[0], 0, 0)),
                  vec, vec, vec,
                  pl.BlockSpec((POOL_WIDTH + GROUP_WIDTH, D_MODEL), const, **resident),
                  pl.BlockSpec((D_MODEL, 2 * D_FF), const, **resident),
                  pl.BlockSpec((3, D_FF), const),
                  pl.BlockSpec((1, D_FF), const),
                  pl.BlockSpec((D_FF, D_MODEL), const, **resident)],
        out_specs=pl.BlockSpec((1, tm, D_MODEL), row),
        out_shape=jax.ShapeDtypeStruct((batch, seq, D_MODEL), F32),
        scratch_shapes=[pltpu.VMEM((tm, D_FF), BF16),
                        pltpu.VMEM((SUBLANES, D_FF), F32),
                        pltpu.VMEM((tm, D_MODEL), F32),
                        pltpu.VMEM((tm, D_MODEL), F32),
                        pltpu.VMEM((tm, D_MODEL), BF16),
                        pltpu.VMEM((tm, D_MODEL), BF16),
                        pltpu.VMEM((2, 2, tm, FF_CHUNK), F32)],
        compiler_params=_params(2),
        name="ffn",
    )(x, pool, attn, x, pool, attn, mod3, mod3, g_post_mix, g_pre_ffn, g_post_ffn, w_out, w_up, conv_w,
      conv_b, w_down)


def kernel(x, c, positions, w_ada, b_ada, g_pre_mix, g_post_mix, g_pre_ffn, g_post_ffn,
           w_in, w_pool, b_pool, pool_scale, w_out, w_up, conv_w, conv_b, w_down):
    depth = w_ada.shape[0]
    batch = x.shape[0]
    for l in range(depth):
        mod, w_in_bf = _mod_call(c, w_ada[l], b_ada[l], w_in[l])
        mod3 = mod.reshape(batch, N_MOD, D_MODEL)
        wpool_blk = jax.scipy.linalg.block_diag(*[w_pool[l, g] for g in range(len(POOL_WINDOWS))])
        pool, *qkv = _inproj_call(
            x, mod3, g_pre_mix[l].reshape(1, D_MODEL), w_in_bf, positions,
            wpool_blk.astype(BF16), b_pool[l].reshape(1, POOL_WIDTH), pool_scale[l].reshape(1, POOL_WIDTH))
        attn, w_out_bf, w_up_bf, w_down_bf = _attn_call([qkv[0:3], qkv[3:6], qkv[6:9]],
                                                         [w_out[l], w_up[l], w_down[l]])
        x = _ffn_call(x, pool, attn, mod3, g_post_mix[l].reshape(1, D_MODEL), g_pre_ffn[l].reshape(1, D_MODEL),
                      g_post_ffn[l].reshape(1, D_MODEL), w_out_bf, w_up_bf, conv_w[l],
                      conv_b[l].reshape(1, D_FF), w_down_bf)
    return x
```

```python
import numpy as np

import jax
import jax.numpy as jnp
from jax import lax
from jax.experimental import pallas as pl
from jax.experimental.pallas import tpu as pltpu

F32 = jnp.float32
BF16 = jnp.bfloat16

D_MODEL = 1024
HEAD_DIM = 64
POOL_WIDTH = 256
POOL_WINDOWS = (2, 4, 8, 16)
POOL_GROUP_DIM = 64
MAX_POOL_WINDOW = 16
ATTN_WIDTH = 768
GROUP_WIDTH = 256
HEADS_PER_GROUP = 4
DILATIONS = (1, 4, 16)
DESTRIDE_STEP = 4
SPAN = 128
BLOCK = 128
IN_PROJ_WIDTH = 2560
ROT_DIM = 16
ROT_HALF = 8
ROPE_THETA = 500000.0
D_FF = 2816
NORM_EPS = 1e-6
N_MOD = 6
MASK_VALUE = -1e30

LANES = 128
SUBLANES = 8
VMEM_LIMIT_BYTES = 56 * 1024 * 1024

ROW_TILE = 512
FFN_PIECE = 128
DOWN_CHUNK = 256
INPROJ_TILE = 1024
INPROJ_SUB = 512
PIECE = 64
FF_CHUNK = 256
ATTN_CHUNK = BLOCK * DILATIONS[-1]
HALVES = GROUP_WIDTH // LANES
N_SPLIT = 3


def _params(n_axes, vmem=VMEM_LIMIT_BYTES):
    return pltpu.CompilerParams(dimension_semantics=("arbitrary",) * n_axes, vmem_limit_bytes=vmem)


def _rms_scale(v):
    return lax.rsqrt(jnp.mean(v * v, axis=-1, keepdims=True) + NORM_EPS)


def _mod_kernel(c_ref, w_ref, b_ref, win_ref, o_ref, win_bf_ref):
    c = c_ref[...]
    c_act = c * jax.nn.sigmoid(c)
    o_ref[...] = jnp.dot(c_act.astype(BF16), w_ref[...].astype(BF16),
                         preferred_element_type=F32) + b_ref[...]
    win_bf_ref[...] = win_ref[...].astype(BF16)


def _mod_call(c, w_ada, b_ada, w_in):
    batch = c.shape[0]
    n = w_ada.shape[1]
    tn = 768
    steps = n // tn
    slab = pl.BlockSpec((w_in.shape[0] // steps, w_in.shape[1]), lambda j: (j, 0))
    return pl.pallas_call(
        _mod_kernel,
        grid=(steps,),
        in_specs=[pl.BlockSpec((batch, D_MODEL), lambda j: (0, 0)),
                  pl.BlockSpec((D_MODEL, tn), lambda j: (0, j)),
                  pl.BlockSpec((1, tn), lambda j: (0, j)),
                  slab],
        out_specs=[pl.BlockSpec((batch, tn), lambda j: (0, j)), slab],
        out_shape=[jax.ShapeDtypeStruct((batch, n), F32), jax.ShapeDtypeStruct(w_in.shape, BF16)],
        compiler_params=_params(1),
        name="mod",
    )(c, w_ada, b_ada.reshape(1, n), w_in)


def _rope_expansion():
    e = np.zeros((LANES, 2 * LANES), np.float32)
    ones_row = 2 * N_SPLIT * ROT_HALF
    for lane in range(LANES):
        in_head = lane % HEAD_DIM
        if in_head >= ROT_DIM:
            e[ones_row, lane] = 1.0
            continue
        sign = -1.0 if in_head < ROT_HALF else 1.0
        for t in range(N_SPLIT):
            e[t * ROT_HALF + in_head % ROT_HALF, lane] = 1.0
            e[(N_SPLIT + t) * ROT_HALF + in_head % ROT_HALF, LANES + lane] = sign
    return e


def _rope_terms(pos_row, freq_col):
    ang = freq_col * pos_row
    terms = []
    for table in (jnp.cos(ang), jnp.sin(ang)):
        rest = table
        for _ in range(N_SPLIT):
            part = rest.astype(BF16).astype(F32)
            terms.append(part)
            rest = rest - part
    terms.append(jnp.ones((SUBLANES, LANES), F32))
    pad = jnp.zeros((LANES - SUBLANES * len(terms), LANES), F32)
    return jnp.concatenate(terms + [pad], axis=0).T.astype(BF16)


def _rope_expand(terms, expand):
    both = jnp.dot(terms, expand, preferred_element_type=F32)
    return both[:, :LANES], both[:, LANES:]


def _inproj_kernel(x_ref, xn_ref, mod_ref, modn_ref, g_ref, w_ref, pos_ref, posn_ref, freq_ref, expand_ref,
                   wpool_ref, bpool_ref, pscale_ref,
                   pool_ref, q0_ref, k0_ref, v0_ref, q1_ref, k1_ref, v1_ref, q2_ref, k2_ref, v2_ref,
                   carry_ref, stage_ref, mid_ref, h_ref, tab_ref, raw_ref, mixed_ref, terms_ref):
    s = pl.program_id(1)
    tm = x_ref.shape[1]
    sub = INPROJ_SUB
    q_scale = HEAD_DIM ** -0.5
    first_half = (lax.broadcasted_iota(jnp.int32, (sub, LANES), 1) % HEAD_DIM) < ROT_HALF
    low_group = lax.broadcasted_iota(jnp.int32, (sub + MAX_POOL_WINDOW, LANES), 1) < POOL_GROUP_DIM
    low_group_head = lax.broadcasted_iota(jnp.int32, (MAX_POOL_WINDOW, LANES), 1) < POOL_GROUP_DIM
    low_group_row = lax.broadcasted_iota(jnp.int32, (1, LANES), 1) < POOL_GROUP_DIM
    out_refs = ((q0_ref, k0_ref, v0_ref), (q1_ref, k1_ref, v1_ref), (q2_ref, k2_ref, v2_ref))

    @pl.when(s == 0)
    def _():
        carry_ref[...] = jnp.zeros_like(carry_ref)

    def rope(t, cos, sin):
        partner = jnp.where(first_half, pltpu.roll(t, LANES - ROT_HALF, 1), pltpu.roll(t, ROT_HALF, 1))
        return t * cos + partner * sin

    def emit(out_ref, value, d, half, a, slot):
        cols = slice(half * LANES, (half + 1) * LANES)
        rows_out = sub // d
        out_rows = slice(a * rows_out, (a + 1) * rows_out)
        if d == 1:
            out_ref[0, 0, out_rows, cols] = value.astype(BF16)
            return
        stage_ref[slot] = value
        step = DESTRIDE_STEP
        if d == step:
            for r in range(d):
                picked = stage_ref[slot, pl.ds(r, rows_out, stride=step), :]
                out_ref[0, r, out_rows, cols] = picked.astype(BF16)
            return
        part = sub // step
        for r_lo in range(step):
            mid_ref[slot, r_lo * part:(r_lo + 1) * part, :] = stage_ref[slot, pl.ds(r_lo, part, stride=step), :]
        for r_lo in range(step):
            for r_hi in range(step):
                r = r_lo + step * r_hi
                picked = mid_ref[slot, pl.ds(r_lo * part + r_hi, rows_out, stride=step), :]
                out_ref[0, r, out_rows, cols] = picked.astype(BF16)

    def pool_mix(u, row0):
        ext = jnp.concatenate([carry_ref[...], u], axis=0)
        carry_ref[...] = u[sub - MAX_POOL_WINDOW:, :]
        head = MAX_POOL_WINDOW
        head_row = lax.broadcasted_iota(jnp.int32, (head, LANES), 0) + row0 + 1
        mixed = []
        for half in range(POOL_WIDTH // LANES):
            w_a, w_b = POOL_WINDOWS[2 * half], POOL_WINDOWS[2 * half + 1]
            acc = ext[:, half * LANES:(half + 1) * LANES]
            width = 1
            while width < w_a:
                acc = acc + pltpu.roll(acc, width, 0)
                width *= 2
            sum_a = acc
            while width < w_b:
                acc = acc + pltpu.roll(acc, width, 0)
                width *= 2
            win = jnp.where(low_group, sum_a, acc)[MAX_POOL_WINDOW:, :]
            window = jnp.where(low_group_head, w_a, w_b)
            count = jnp.minimum(head_row, window).astype(F32)
            inv_window = jnp.where(low_group_row, 1.0 / w_a, 1.0 / w_b)
            mean = jnp.concatenate([win[:head] / count, win[head:] * inv_window], axis=0)
            mixed.append(mean - u[:, half * LANES:(half + 1) * LANES])
        mixed_ref[...] = jnp.concatenate(mixed, axis=1).astype(BF16)

    def pool_finish(rows):
        y = jnp.dot(mixed_ref[...], wpool_ref[...], preferred_element_type=F32) + bpool_ref[...]
        pool_ref[0, rows, :] = (y * pscale_ref[...]).astype(BF16)

    def prologue_piece(k, dst, xsrc_ref, row0, psrc_ref, prow0, gain, shift):
        rows = slice(k * PIECE, (k + 1) * PIECE)
        x = xsrc_ref[0, row0 + k * PIECE:row0 + (k + 1) * PIECE, :]
        h_ref[dst, rows, :] = (x * _rms_scale(x) * gain + shift).astype(BF16)
        if 1 <= k <= sub // LANES:
            cos, sin = _rope_expand(terms_ref[k - 1], expand_ref[...])
            for i, table in enumerate((cos, sin, cos * q_scale, sin * q_scale)):
                tab_ref[dst, i, (k - 1) * LANES:k * LANES, :] = table
        if k < sub // LANES:
            pos = psrc_ref[0, 0, prow0 + k:prow0 + k + 1, :].astype(F32)
            terms_ref[k] = _rope_terms(pos, freq_ref[...])

    n_sub = tm // sub
    n_pieces = sub // PIECE
    shift = mod_ref[0, 0:1, :]
    gain = g_ref[...] * (1.0 + mod_ref[0, 1:2, :])
    shift_next = modn_ref[0, 0:1, :]
    gain_next = g_ref[...] * (1.0 + modn_ref[0, 1:2, :])

    @pl.when((pl.program_id(0) == 0) & (s == 0))
    def _():
        for k in range(n_pieces):
            prologue_piece(k, 0, x_ref, 0, pos_ref, 0, gain, shift)

    chunks = [(g, kind) for g in range(len(DILATIONS)) for kind in range(3)]
    chunks.remove((0, 2))
    chunks.append((None, None))
    chunks.append((0, 2))
    slots = {}
    for g, d in enumerate(DILATIONS):
        if d > 1:
            for kind in range(3):
                for half in range(HALVES):
                    slots[g, kind, half] = len(slots)

    for a in range(n_sub):
        rows = slice(a * sub, (a + 1) * sub)
        h = h_ref[a % 2]
        cos, sin, cos_q, sin_q = (tab_ref[a % 2, i] for i in range(4))
        for ci, (g, kind) in enumerate(chunks):
            if g is None:
                raw_ref[ci % 2] = jnp.dot(h, w_ref[:, :POOL_WIDTH], preferred_element_type=F32)
                pool_mix(raw_ref[ci % 2], s * tm + a * sub)
            else:
                lo = POOL_WIDTH + kind * ATTN_WIDTH + g * GROUP_WIDTH
                raw_ref[ci % 2] = jnp.dot(h, w_ref[:, lo:lo + GROUP_WIDTH], preferred_element_type=F32)
                for half in range(HALVES):
                    value = raw_ref[ci % 2, :, half * LANES:(half + 1) * LANES]
                    if kind == 0:
                        value = rope(value, cos_q, sin_q)
                    elif kind == 1:
                        value = rope(value, cos, sin)
                    emit(out_refs[g][kind], value, DILATIONS[g], half, a, slots.get((g, kind, half)))
            if ci > 0 and chunks[ci - 1][0] is None:
                pool_finish(rows)
            if ci < n_pieces:
                if a + 1 < n_sub:
                    prologue_piece(ci, (a + 1) % 2, x_ref, (a + 1) * sub, pos_ref, (a + 1) * (sub // LANES),
                                   gain, shift)
                else:
                    prologue_piece(ci, (a + 1) % 2, xn_ref, 0, posn_ref, 0, gain_next, shift_next)


def _inproj_call(x, mod3, g_pre, w_in, positions, wpool_blk, bpool, pscale):
    batch, seq, _ = x.shape
    tm = INPROJ_TILE
    row = lambda b, s: (b, s, 0)
    const = lambda b, s: (0, 0)
    resident = dict(pipeline_mode=pl.Buffered(1))
    inv_freq = ROPE_THETA ** (-jnp.arange(0, ROT_DIM, 2, dtype=F32) / ROT_DIM)
    pos4 = positions.reshape(batch, seq // tm, tm // LANES, LANES)
    expand = jnp.asarray(_rope_expansion(), BF16)
    steps = seq // tm

    def next_step(b, s):
        wrap = (s + 1) // steps
        return jnp.minimum(b + wrap, batch - 1), (s + 1) % steps

    out_specs = [pl.BlockSpec((1, tm, GROUP_WIDTH), row)]
    out_shape = [jax.ShapeDtypeStruct((batch, seq, GROUP_WIDTH), BF16)]
    n_stage = 0
    for d in DILATIONS:
        out_specs += [pl.BlockSpec((1, d, tm // d, GROUP_WIDTH), lambda b, s: (b, 0, s, 0))] * 3
        out_shape += [jax.ShapeDtypeStruct((batch, d, seq // d, GROUP_WIDTH), BF16)] * 3
        n_stage += 3 * HALVES if d > 1 else 0
    return pl.pallas_call(
        _inproj_kernel,
        grid=(batch, seq // tm),
        in_specs=[pl.BlockSpec((1, tm, D_MODEL), row),
                  pl.BlockSpec((1, INPROJ_SUB, D_MODEL),
                               lambda b, s: (next_step(b, s)[0], next_step(b, s)[1] * (tm // INPROJ_SUB), 0)),
                  pl.BlockSpec((1, N_MOD, D_MODEL), lambda b, s: (b, 0, 0)),
                  pl.BlockSpec((1, N_MOD, D_MODEL), lambda b, s: (next_step(b, s)[0], 0, 0)),
                  pl.BlockSpec((1, D_MODEL), const),
                  pl.BlockSpec((D_MODEL, IN_PROJ_WIDTH), const, **resident),
                  pl.BlockSpec((1, 1, tm // LANES, LANES), lambda b, s: (b, s, 0, 0)),
                  pl.BlockSpec((1, 1, tm // LANES, LANES), lambda b, s: (*next_step(b, s), 0, 0)),
                  pl.BlockSpec((ROT_HALF, 1), const),
                  pl.BlockSpec((LANES, 2 * LANES), const),
                  pl.BlockSpec((POOL_WIDTH, POOL_WIDTH), const),
                  pl.BlockSpec((1, POOL_WIDTH), const),
                  pl.BlockSpec((1, POOL_WIDTH), const)],
        out_specs=out_specs,
        out_shape=out_shape,
        scratch_shapes=[pltpu.VMEM((MAX_POOL_WINDOW, POOL_WIDTH), F32),
                        pltpu.VMEM((n_stage, INPROJ_SUB, LANES), F32),
                        pltpu.VMEM((n_stage, INPROJ_SUB, LANES), F32),
                        pltpu.VMEM((2, INPROJ_SUB, D_MODEL), BF16),
                        pltpu.VMEM((2, 4, INPROJ_SUB, LANES), F32),
                        pltpu.VMEM((2, INPROJ_SUB, GROUP_WIDTH), F32),
                        pltpu.VMEM((INPROJ_SUB, POOL_WIDTH), BF16),
                        pltpu.VMEM((INPROJ_SUB // LANES, LANES, LANES), BF16)],
        compiler_params=_params(2),
        name="inproj",
    )(x, x, mod3, mod3, g_pre, w_in, pos4, pos4, inv_freq.reshape(ROT_HALF, 1), expand, wpool_blk, bpool,
      pscale)


def _attend(q, keys, vals, bias, head_masks):
    q_heads = jnp.concatenate([jnp.where(m, q, jnp.zeros_like(q)) for m in head_masks], axis=0)
    scores = lax.dot_general(q_heads, keys, (((1,), (1,)), ((), ())), preferred_element_type=F32)
    probs, lses = [], []
    for h in range(HEADS_PER_GROUP):
        s = scores[h * BLOCK:(h + 1) * BLOCK] + bias
        m_row = jnp.max(s, axis=-1, keepdims=True)
        p = jnp.exp(s - m_row)
        den = jnp.sum(p, axis=-1, keepdims=True)
        probs.append((p * (1.0 / den)).astype(BF16))
        lses.append(m_row + jnp.log(den))
    pv = jnp.dot(jnp.concatenate(probs, axis=0), vals, preferred_element_type=F32)
    out = pv[(HEADS_PER_GROUP - 1) * BLOCK:]
    lse_out = jnp.broadcast_to(lses[-1], (BLOCK, GROUP_WIDTH))
    for h in range(HEADS_PER_GROUP - 2, -1, -1):
        out = jnp.where(head_masks[h], pv[h * BLOCK:(h + 1) * BLOCK], out)
        lse_out = jnp.where(head_masks[h], lses[h], lse_out)
    return out, lse_out


def _attn_kernel(q0_ref, kp0_ref, kc0_ref, vp0_ref, vc0_ref,
                 q1_ref, kp1_ref, kc1_ref, vp1_ref, vc1_ref,
                 q2_ref, kp2_ref, kc2_ref, vp2_ref, vc2_ref, wout_ref, wup_ref, wdown_ref,
                 o_ref, wout_bf_ref, wup_bf_ref, wdown_bf_ref, ost_ref, lst_ref):
    for src_ref, dst_ref in ((wout_ref, wout_bf_ref), (wup_ref, wup_bf_ref), (wdown_ref, wdown_bf_ref)):
        dst_ref[...] = src_ref[...].astype(BF16)
    first_chunk = pl.program_id(1) == 0
    i = lax.broadcasted_iota(jnp.int32, (BLOCK, 2 * BLOCK), 0)
    j = lax.broadcasted_iota(jnp.int32, (BLOCK, 2 * BLOCK), 1)
    valid = (j >= i) & (j <= i + SPAN)
    bias = jnp.where(valid, 0.0, MASK_VALUE)
    has_prev = jnp.logical_not(first_chunk)
    bias_first = jnp.where(valid & ((j >= BLOCK) | has_prev), 0.0, MASK_VALUE)
    lane = lax.broadcasted_iota(jnp.int32, (BLOCK, GROUP_WIDTH), 1)
    head_masks = [(lane // HEAD_DIM) == h for h in range(HEADS_PER_GROUP)]
    groups = ((q0_ref, kp0_ref, kc0_ref, vp0_ref, vc0_ref),
              (q1_ref, kp1_ref, kc1_ref, vp1_ref, vc1_ref),
              (q2_ref, kp2_ref, kc2_ref, vp2_ref, vc2_ref))

    def first_tile(g, r):
        q_ref, kp_ref, kc_ref, vp_ref, vc_ref = groups[g]
        keys = jnp.concatenate([kp_ref[0, r], kc_ref[0, r, :BLOCK, :]], axis=0)
        vals = jnp.concatenate([vp_ref[0, r], vc_ref[0, r, :BLOCK, :]], axis=0)
        return _attend(q_ref[0, r, :BLOCK, :], keys, vals, bias_first, head_masks)

    def later_tile(g, r, n):
        q_ref, _, kc_ref, _, vc_ref = groups[g]
        start = (n - 1) * BLOCK
        both = pl.ds(start, 2 * BLOCK)
        return _attend(q_ref[0, r, pl.ds(start + BLOCK, BLOCK), :], kc_ref[0, r, both, :],
                       vc_ref[0, r, both, :], bias, head_masks)

    def token_rows(g, r, n):
        d = DILATIONS[g]
        start = n * (d * BLOCK) + r
        return pl.ds(start, BLOCK, stride=d) if d > 1 else pl.ds(start, BLOCK)

    def merged(rows, out, lse, half):
        cols = slice(half * LANES, (half + 1) * LANES)
        o_a, l_a, o_b, l_b = ost_ref[half, rows, :], lst_ref[half, rows, :], out[:, cols], lse[:, cols]
        top = jnp.maximum(l_a, l_b)
        w_a = jnp.exp(l_a - top)
        w_b = jnp.exp(l_b - top)
        total = w_a + w_b
        return (w_a * o_a + w_b * o_b) * (1.0 / total), top + jnp.log(total)

    def start_state(rows, out, lse):
        for half in range(HALVES):
            cols = slice(half * LANES, (half + 1) * LANES)
            ost_ref[half, rows, :] = out[:, cols]
            lst_ref[half, rows, :] = lse[:, cols]

    def fold_state(rows, out, lse):
        for half in range(HALVES):
            ost_ref[half, rows, :], lst_ref[half, rows, :] = merged(rows, out, lse, half)

    def finish(rows, out, lse):
        for half in range(HALVES):
            o_ref[0, rows, half * LANES:(half + 1) * LANES] = merged(rows, out, lse, half)[0].astype(BF16)

    blocks = [ATTN_CHUNK // (d * BLOCK) for d in DILATIONS]
    for r in range(DILATIONS[2]):
        start_state(token_rows(2, r, 0), *first_tile(2, r))
    for r in range(DILATIONS[1]):
        fold_state(token_rows(1, r, 0), *first_tile(1, r))
        for n in range(1, blocks[1]):
            fold_state(token_rows(1, r, n), *later_tile(1, r, n))
    finish(token_rows(0, 0, 0), *first_tile(0, 0))
    for n in range(1, blocks[0]):
        finish(token_rows(0, 0, n), *later_tile(0, 0, n))


def _attn_call(qkv, weights):
    batch = qkv[0][0].shape[0]
    seq = qkv[0][0].shape[2]
    chunks = seq // ATTN_CHUNK
    n_steps = batch * chunks
    in_specs, args = [], []
    for (q, k, v), d in zip(qkv, DILATIONS):
        rows = ATTN_CHUNK // d
        cur = pl.BlockSpec((1, d, rows, GROUP_WIDTH), lambda b, c: (b, 0, c, 0))
        blocks_per_chunk = rows // BLOCK
        prev = pl.BlockSpec((1, d, BLOCK, GROUP_WIDTH),
                            lambda b, c, n=blocks_per_chunk: (b, 0, jnp.maximum(c * n - 1, 0), 0))
        in_specs += [cur, prev, cur, prev, cur]
        args += [q, k, k, v, v]
    weight_specs = [pl.BlockSpec((w.shape[0] // n_steps, w.shape[1]), lambda b, c: (b * chunks + c, 0))
                    for w in weights]
    return pl.pallas_call(
        _attn_kernel,
        grid=(batch, chunks),
        in_specs=in_specs + weight_specs,
        out_specs=[pl.BlockSpec((1, ATTN_CHUNK, GROUP_WIDTH), lambda b, c: (b, c, 0))] + weight_specs,
        out_shape=[jax.ShapeDtypeStruct((batch, seq, GROUP_WIDTH), BF16)]
                  + [jax.ShapeDtypeStruct(w.shape, BF16) for w in weights],
        scratch_shapes=[pltpu.VMEM((HALVES, ATTN_CHUNK, LANES), F32),
                        pltpu.VMEM((HALVES, ATTN_CHUNK, LANES), F32)],
        compiler_params=_params(2),
        name="attn",
    )(*args, *weights)


def _ffn_kernel(x_ref, pool_ref, attn_ref, xn_ref, pooln_ref, attnn_ref, mod_ref, modn_ref,
                gpostm_ref, gpref_ref, gpostf_ref, wout_ref, wup_ref, cw_ref, cb_ref, wdown_ref,
                out_ref, act_ref, carry_ref, x1a_ref, x1b_ref, h2a_ref, h2b_ref, raw_ref):
    s = pl.program_id(1)
    tm = out_ref.shape[1]
    halo = carry_ref.shape[0]
    n_pieces = tm // FFN_PIECE

    @pl.when(s == 0)
    def _():
        carry_ref[...] = jnp.zeros_like(carry_ref)

    def prologue_piece(k, x1_dst, h2_dst, xs_ref, ps_ref, as_ref, ms_ref):
        rows = slice(k * FFN_PIECE, (k + 1) * FFN_PIECE)
        mixed = jnp.concatenate([ps_ref[0, rows, :], as_ref[0, rows, :]], axis=-1)
        y = jnp.dot(mixed, wout_ref[...], preferred_element_type=F32)
        x1 = xs_ref[0, rows, :] + ms_ref[0, 2:3, :] * (y * _rms_scale(y) * gpostm_ref[...])
        x1_dst[rows, :] = x1
        h2 = x1 * _rms_scale(x1) * gpref_ref[...] * (1.0 + ms_ref[0, 4:5, :]) + ms_ref[0, 3:4, :]
        h2_dst[rows, :] = h2.astype(BF16)

    def step(x1_cur, h2_cur, x1_next, h2_next):
        h2 = h2_cur[...]
        n_chunks = D_FF // FF_CHUNK
        for c in range(n_chunks):
            cols = slice(c * FF_CHUNK, (c + 1) * FF_CHUNK)
            vcols = slice(D_FF + c * FF_CHUNK, D_FF + (c + 1) * FF_CHUNK)
            raw_ref[c % 2, 0] = jnp.dot(h2, wup_ref[:, cols], preferred_element_type=F32)
            raw_ref[c % 2, 1] = jnp.dot(h2, wup_ref[:, vcols], preferred_element_type=F32)
            gate, val = raw_ref[c % 2, 0], raw_ref[c % 2, 1]
            ext = jnp.concatenate([carry_ref[:, cols], gate], axis=0)
            carry_ref[:, cols] = gate[tm - halo:, :]
            back1 = pltpu.roll(ext, 1, 0)[halo:, :]
            back2 = pltpu.roll(ext, 2, 0)[halo:, :]
            conv = (back2 * cw_ref[0:1, cols] + back1 * cw_ref[1:2, cols] + gate * cw_ref[2:3, cols]
                    + cb_ref[:, cols])
            act_ref[:, cols] = (jax.nn.gelu(conv, approximate=True) * val).astype(BF16)
        act = act_ref[...]
        y_parts = []
        for k in range(D_MODEL // DOWN_CHUNK):
            if k < n_pieces:
                prologue_piece(k, x1_next, h2_next, xn_ref, pooln_ref, attnn_ref, modn_ref)
            y_parts.append(jnp.dot(act, wdown_ref[:, k * DOWN_CHUNK:(k + 1) * DOWN_CHUNK],
                                   preferred_element_type=F32))
        y = jnp.concatenate(y_parts, axis=-1)
        out_ref[0] = x1_cur[...] + mod_ref[0, 5:6, :] * (y * _rms_scale(y) * gpostf_ref[...])

    @pl.when((pl.program_id(0) == 0) & (s == 0))
    def _():
        for k in range(n_pieces):
            prologue_piece(k, x1a_ref, h2a_ref, x_ref, pool_ref, attn_ref, mod_ref)

    @pl.when(s % 2 == 0)
    def _():
        step(x1a_ref, h2a_ref, x1b_ref, h2b_ref)

    @pl.when(s % 2 == 1)
    def _():
        step(x1b_ref, h2b_ref, x1a_ref, h2a_ref)


def _ffn_call(x, pool, attn, mod3, g_post_mix, g_pre_ffn, g_post_ffn, w_out, w_up, conv_w, conv_b, w_down):
    batch, seq, _ = x.shape
    tm = ROW_TILE
    steps = seq // tm
    assert steps % 2 == 0
    row = lambda b, s: (b, s, 0)
    const = lambda b, s: (0, 0)

    def nxt(b, s):
        return jnp.minimum(b + (s + 1) // steps, batch - 1), (s + 1) % steps, 0

    resident = dict(pipeline_mode=pl.Buffered(1))
    vec = pl.BlockSpec((1, D_MODEL), const)
    return pl.pallas_call(
        _ffn_kernel,
        grid=(batch, steps),
        in_specs=[pl.BlockSpec((1, tm, D_MODEL), row),
                  pl.BlockSpec((1, tm, GROUP_WIDTH), row),
                  pl.BlockSpec((1, tm, GROUP_WIDTH), row),
                  pl.BlockSpec((1, tm, D_MODEL), nxt),
                  pl.BlockSpec((1, tm, GROUP_WIDTH), nxt),
                  pl.BlockSpec((1, tm, GROUP_WIDTH), nxt),
                  pl.BlockSpec((1, N_MOD, D_MODEL), lambda b, s: (b, 0, 0)),
                  pl.BlockSpec((1, N_MOD, D_MODEL), lambda b, s: (nxt(b, s)[0], 0, 0)),
                  vec, vec, vec,
                  pl.BlockSpec((POOL_WIDTH + GROUP_WIDTH, D_MODEL), const, **resident),
                  pl.BlockSpec((D_MODEL, 2 * D_FF), const, **resident),
                  pl.BlockSpec((3, D_FF), const),
                  pl.BlockSpec((1, D_FF), const),
                  pl.BlockSpec((D_FF, D_MODEL), const, **resident)],
        out_specs=pl.BlockSpec((1, tm, D_MODEL), row),
        out_shape=jax.ShapeDtypeStruct((batch, seq, D_MODEL), F32),
        scratch_shapes=[pltpu.VMEM((tm, D_FF), BF16),
                        pltpu.VMEM((SUBLANES, D_FF), F32),
                        pltpu.VMEM((tm, D_MODEL), F32),
                        pltpu.VMEM((tm, D_MODEL), F32),
                        pltpu.VMEM((tm, D_MODEL), BF16),
                        pltpu.VMEM((tm, D_MODEL), BF16),
                        pltpu.VMEM((2, 2, tm, FF_CHUNK), F32)],
        compiler_params=_params(2),
        name="ffn",
    )(x, pool, attn, x, pool, attn, mod3, mod3, g_post_mix, g_pre_ffn, g_post_ffn, w_out, w_up, conv_w,
      conv_b, w_down)


def kernel(x, c, positions, w_ada, b_ada, g_pre_mix, g_post_mix, g_pre_ffn, g_post_ffn,
           w_in, w_pool, b_pool, pool_scale, w_out, w_up, conv_w, conv_b, w_down):
    depth = w_ada.shape[0]
    batch = x.shape[0]
    for l in range(depth):
        mod, w_in_bf = _mod_call(c, w_ada[l], b_ada[l], w_in[l])
        mod3 = mod.reshape(batch, N_MOD, D_MODEL)
        wpool_blk = jax.scipy.linalg.block_diag(*[w_pool[l, g] for g in range(len(POOL_WINDOWS))])
        pool, *qkv = _inproj_call(
            x, mod3, g_pre_mix[l].reshape(1, D_MODEL), w_in_bf, positions,
            wpool_blk.astype(BF16), b_pool[l].reshape(1, POOL_WIDTH), pool_scale[l].reshape(1, POOL_WIDTH))
        attn, w_out_bf, w_up_bf, w_down_bf = _attn_call([qkv[0:3], qkv[3:6], qkv[6:9]],
                                                         [w_out[l], w_up[l], w_down[l]])
        x = _ffn_call(x, pool, attn, mod3, g_post_mix[l].reshape(1, D_MODEL), g_pre_ffn[l].reshape(1, D_MODEL),
                      g_post_ffn[l].reshape(1, D_MODEL), w_out_bf, w_up_bf, conv_w[l],
                      conv_b[l].reshape(1, D_FF), w_down_bf)
    return x
```

```python
import numpy as np

import jax
import jax.numpy as jnp
from jax import lax
from jax.experimental import pallas as pl
from jax.experimental.pallas import tpu as pltpu

F32 = jnp.float32
BF16 = jnp.bfloat16

D_MODEL = 1024
HEAD_DIM = 64
POOL_WIDTH = 256
POOL_WINDOWS = (2, 4, 8, 16)
POOL_GROUP_DIM = 64
MAX_POOL_WINDOW = 16
ATTN_WIDTH = 768
GROUP_WIDTH = 256
HEADS_PER_GROUP = 4
DILATIONS = (1, 4, 16)
DESTRIDE_STEP = 4
SPAN = 128
BLOCK = 128
IN_PROJ_WIDTH = 2560
ROT_DIM = 16
ROT_HALF = 8
ROPE_THETA = 500000.0
D_FF = 2816
NORM_EPS = 1e-6
N_MOD = 6
MASK_VALUE = -1e30

LANES = 128
SUBLANES = 8
VMEM_LIMIT_BYTES = 56 * 1024 * 1024

ROW_TILE = 512
FFN_PIECE = 128
DOWN_CHUNK = 256
INPROJ_TILE = 1024
INPROJ_SUB = 512
PIECE = 64
FF_CHUNK = 256
ATTN_CHUNK = BLOCK * DILATIONS[-1]
HALVES = GROUP_WIDTH // LANES
N_SPLIT = 3


def _params(n_axes, vmem=VMEM_LIMIT_BYTES):
    return pltpu.CompilerParams(dimension_semantics=("arbitrary",) * n_axes, vmem_limit_bytes=vmem)


def _rms_scale(v):
    return lax.rsqrt(jnp.mean(v * v, axis=-1, keepdims=True) + NORM_EPS)


def _mod_kernel(c_ref, w_ref, b_ref, win_ref, o_ref, win_bf_ref):
    c = c_ref[...]
    c_act = c * jax.nn.sigmoid(c)
    o_ref[...] = jnp.dot(c_act.astype(BF16), w_ref[...].astype(BF16),
                         preferred_element_type=F32) + b_ref[...]
    win_bf_ref[...] = win_ref[...].astype(BF16)


def _mod_call(c, w_ada, b_ada, w_in):
    batch = c.shape[0]
    n = w_ada.shape[1]
    tn = 768
    steps = n // tn
    slab = pl.BlockSpec((w_in.shape[0] // steps, w_in.shape[1]), lambda j: (j, 0))
    return pl.pallas_call(
        _mod_kernel,
        grid=(steps,),
        in_specs=[pl.BlockSpec((batch, D_MODEL), lambda j: (0, 0)),
                  pl.BlockSpec((D_MODEL, tn), lambda j: (0, j)),
                  pl.BlockSpec((1, tn), lambda j: (0, j)),
                  slab],
        out_specs=[pl.BlockSpec((batch, tn), lambda j: (0, j)), slab],
        out_shape=[jax.ShapeDtypeStruct((batch, n), F32), jax.ShapeDtypeStruct(w_in.shape, BF16)],
        compiler_params=_params(1),
        name="mod",
    )(c, w_ada, b_ada.reshape(1, n), w_in)


def _rope_expansion():
    e = np.zeros((LANES, 2 * LANES), np.float32)
    ones_row = 2 * N_SPLIT * ROT_HALF
    for lane in range(LANES):
        in_head = lane % HEAD_DIM
        if in_head >= ROT_DIM:
            e[ones_row, lane] = 1.0
            continue
        sign = -1.0 if in_head < ROT_HALF else 1.0
        for t in range(N_SPLIT):
            e[t * ROT_HALF + in_head % ROT_HALF, lane] = 1.0
            e[(N_SPLIT + t) * ROT_HALF + in_head % ROT_HALF, LANES + lane] = sign
    return e


def _rope_terms(pos_row, freq_col):
    ang = freq_col * pos_row
    terms = []
    for table in (jnp.cos(ang), jnp.sin(ang)):
        rest = table
        for _ in range(N_SPLIT):
            part = rest.astype(BF16).astype(F32)
            terms.append(part)
            rest = rest - part
    terms.append(jnp.ones((SUBLANES, LANES), F32))
    pad = jnp.zeros((LANES - SUBLANES * len(terms), LANES), F32)
    return jnp.concatenate(terms + [pad], axis=0).T.astype(BF16)


def _rope_expand(terms, expand):
    both = jnp.dot(terms, expand, preferred_element_type=F32)
    return both[:, :LANES], both[:, LANES:]


def _inproj_kernel(x_ref, xn_ref, mod_ref, modn_ref, g_ref, w_ref, pos_ref, posn_ref, freq_ref, expand_ref,
                   wpool_ref, bpool_ref, pscale_ref,
                   pool_ref, q0_ref, k0_ref, v0_ref, q1_ref, k1_ref, v1_ref, q2_ref, k2_ref, v2_ref,
                   carry_ref, stage_ref, mid_ref, h_ref, tab_ref, mixed_ref, terms_ref):
    s = pl.program_id(1)
    tm = x_ref.shape[1]
    sub = INPROJ_SUB
    q_scale = HEAD_DIM ** -0.5
    first_half = (lax.broadcasted_iota(jnp.int32, (sub, LANES), 1) % HEAD_DIM) < ROT_HALF
    low_group = lax.broadcasted_iota(jnp.int32, (sub + MAX_POOL_WINDOW, LANES), 1) < POOL_GROUP_DIM
    low_group_head = lax.broadcasted_iota(jnp.int32, (MAX_POOL_WINDOW, LANES), 1) < POOL_GROUP_DIM
    low_group_row = lax.broadcasted_iota(jnp.int32, (1, LANES), 1) < POOL_GROUP_DIM
    out_refs = ((q0_ref, k0_ref, v0_ref), (q1_ref, k1_ref, v1_ref), (q2_ref, k2_ref, v2_ref))

    @pl.when(s == 0)
    def _():
        carry_ref[...] = jnp.zeros_like(carry_ref)

    def rope(t, cos, sin):
        partner = jnp.where(first_half, pltpu.roll(t, LANES - ROT_HALF, 1), pltpu.roll(t, ROT_HALF, 1))
        return t * cos + partner * sin

    def emit(out_ref, value, d, half, a, slot):
        cols = slice(half * LANES, (half + 1) * LANES)
        rows_out = sub // d
        out_rows = slice(a * rows_out, (a + 1) * rows_out)
        if d == 1:
            out_ref[0, 0, out_rows, cols] = value.astype(BF16)
            return
        stage_ref[slot] = value
        step = DESTRIDE_STEP
        if d == step:
            for r in range(d):
                picked = stage_ref[slot, pl.ds(r, rows_out, stride=step), :]
                out_ref[0, r, out_rows, cols] = picked.astype(BF16)
            return
        part = sub // step
        for r_lo in range(step):
            mid_ref[slot, r_lo * part:(r_lo + 1) * part, :] = stage_ref[slot, pl.ds(r_lo, part, stride=step), :]
        for r_lo in range(step):
            for r_hi in range(step):
                r = r_lo + step * r_hi
                picked = mid_ref[slot, pl.ds(r_lo * part + r_hi, rows_out, stride=step), :]
                out_ref[0, r, out_rows, cols] = picked.astype(BF16)

    def pool_mix(u, row0):
        ext = jnp.concatenate([carry_ref[...], u], axis=0)
        carry_ref[...] = u[sub - MAX_POOL_WINDOW:, :]
        head = MAX_POOL_WINDOW
        head_row = lax.broadcasted_iota(jnp.int32, (head, LANES), 0) + row0 + 1
        mixed = []
        for half in range(POOL_WIDTH // LANES):
            w_a, w_b = POOL_WINDOWS[2 * half], POOL_WINDOWS[2 * half + 1]
            acc = ext[:, half * LANES:(half + 1) * LANES]
            width = 1
            while width < w_a:
                acc = acc + pltpu.roll(acc, width, 0)
                width *= 2
            sum_a = acc
            while width < w_b:
                acc = acc + pltpu.roll(acc, width, 0)
                width *= 2
            win = jnp.where(low_group, sum_a, acc)[MAX_POOL_WINDOW:, :]
            window = jnp.where(low_group_head, w_a, w_b)
            count = jnp.minimum(head_row, window).astype(F32)
            inv_window = jnp.where(low_group_row, 1.0 / w_a, 1.0 / w_b)
            mean = jnp.concatenate([win[:head] / count, win[head:] * inv_window], axis=0)
            mixed.append(mean - u[:, half * LANES:(half + 1) * LANES])
        mixed_ref[...] = jnp.concatenate(mixed, axis=1).astype(BF16)

    def pool_finish(rows):
        y = jnp.dot(mixed_ref[...], wpool_ref[...], preferred_element_type=F32) + bpool_ref[...]
        pool_ref[0, rows, :] = (y * pscale_ref[...]).astype(BF16)

    def prologue_piece(k, dst, xsrc_ref, row0, psrc_ref, prow0, gain, shift):
        rows = slice(k * PIECE, (k + 1) * PIECE)
        x = xsrc_ref[0, row0 + k * PIECE:row0 + (k + 1) * PIECE, :]
        h_ref[dst, rows, :] = (x * _rms_scale(x) * gain + shift).astype(BF16)
        if 1 <= k <= sub // LANES:
            cos, sin = _rope_expand(terms_ref[k - 1], expand_ref[...])
            for i, table in enumerate((cos, sin, cos * q_scale, sin * q_scale)):
                tab_ref[dst, i, (k - 1) * LANES:k * LANES, :] = table
        if k < sub // LANES:
            pos = psrc_ref[0, 0, prow0 + k:prow0 + k + 1, :].astype(F32)
            terms_ref[k] = _rope_terms(pos, freq_ref[...])

    n_sub = tm // sub
    n_pieces = sub // PIECE
    shift = mod_ref[0, 0:1, :]
    gain = g_ref[...] * (1.0 + mod_ref[0, 1:2, :])
    shift_next = modn_ref[0, 0:1, :]
    gain_next = g_ref[...] * (1.0 + modn_ref[0, 1:2, :])

    @pl.when((pl.program_id(0) == 0) & (s == 0))
    def _():
        for k in range(n_pieces):
            prologue_piece(k, 0, x_ref, 0, pos_ref, 0, gain, shift)

    chunks = [(g, kind) for g in range(len(DILATIONS)) for kind in range(3)]
    chunks.remove((0, 2))
    chunks.append((None, None))
    chunks.append((0, 2))
    slots = {}
    for g, d in enumerate(DILATIONS):
        if d > 1:
            for kind in range(3):
                for half in range(HALVES):
                    slots[g, kind, half] = len(slots)

    for a in range(n_sub):
        rows = slice(a * sub, (a + 1) * sub)
        h = h_ref[a % 2]
        cos, sin, cos_q, sin_q = (tab_ref[a % 2, i] for i in range(4))
        for ci, (g, kind) in enumerate(chunks):
            if g is None:
                pool_mix(jnp.dot(h, w_ref[:, :POOL_WIDTH], preferred_element_type=F32), s * tm + a * sub)
            else:
                lo = POOL_WIDTH + kind * ATTN_WIDTH + g * GROUP_WIDTH
                proj = jnp.dot(h, w_ref[:, lo:lo + GROUP_WIDTH], preferred_element_type=F32)
                for half in range(HALVES):
                    value = proj[:, half * LANES:(half + 1) * LANES]
                    if kind == 0:
                        value = rope(value, cos_q, sin_q)
                    elif kind == 1:
                        value = rope(value, cos, sin)
                    emit(out_refs[g][kind], value, DILATIONS[g], half, a, slots.get((g, kind, half)))
            if ci > 0 and chunks[ci - 1][0] is None:
                pool_finish(rows)
            if ci < n_pieces:
                if a + 1 < n_sub:
                    prologue_piece(ci, (a + 1) % 2, x_ref, (a + 1) * sub, pos_ref, (a + 1) * (sub // LANES),
                                   gain, shift)
                else:
                    prologue_piece(ci, (a + 1) % 2, xn_ref, 0, posn_ref, 0, gain_next, shift_next)


def _inproj_call(x, mod3, g_pre, w_in, positions, wpool_blk, bpool, pscale):
    batch, seq, _ = x.shape
    tm = INPROJ_TILE
    row = lambda b, s: (b, s, 0)
    const = lambda b, s: (0, 0)
    resident = dict(pipeline_mode=pl.Buffered(1))
    inv_freq = ROPE_THETA ** (-jnp.arange(0, ROT_DIM, 2, dtype=F32) / ROT_DIM)
    pos4 = positions.reshape(batch, seq // tm, tm // LANES, LANES)
    expand = jnp.asarray(_rope_expansion(), BF16)
    steps = seq // tm

    def next_step(b, s):
        wrap = (s + 1) // steps
        return jnp.minimum(b + wrap, batch - 1), (s + 1) % steps

    out_specs = [pl.BlockSpec((1, tm, GROUP_WIDTH), row)]
    out_shape = [jax.ShapeDtypeStruct((batch, seq, GROUP_WIDTH), BF16)]
    n_stage = 0
    for d in DILATIONS:
        out_specs += [pl.BlockSpec((1, d, tm // d, GROUP_WIDTH), lambda b, s: (b, 0, s, 0))] * 3
        out_shape += [jax.ShapeDtypeStruct((batch, d, seq // d, GROUP_WIDTH), BF16)] * 3
        n_stage += 3 * HALVES if d > 1 else 0
    return pl.pallas_call(
        _inproj_kernel,
        grid=(batch, seq // tm),
        in_specs=[pl.BlockSpec((1, tm, D_MODEL), row),
                  pl.BlockSpec((1, INPROJ_SUB, D_MODEL),
                               lambda b, s: (next_step(b, s)[0], next_step(b, s)[1] * (tm // INPROJ_SUB), 0)),
                  pl.BlockSpec((1, N_MOD, D_MODEL), lambda b, s: (b, 0, 0)),
                  pl.BlockSpec((1, N_MOD, D_MODEL), lambda b, s: (next_step(b, s)[0], 0, 0)),
                  pl.BlockSpec((1, D_MODEL), const),
                  pl.BlockSpec((D_MODEL, IN_PROJ_WIDTH), const, **resident),
                  pl.BlockSpec((1, 1, tm // LANES, LANES), lambda b, s: (b, s, 0, 0)),
                  pl.BlockSpec((1, 1, tm // LANES, LANES), lambda b, s: (*next_step(b, s), 0, 0)),
                  pl.BlockSpec((ROT_HALF, 1), const),
                  pl.BlockSpec((LANES, 2 * LANES), const),
                  pl.BlockSpec((POOL_WIDTH, POOL_WIDTH), const),
                  pl.BlockSpec((1, POOL_WIDTH), const),
                  pl.BlockSpec((1, POOL_WIDTH), const)],
        out_specs=out_specs,
        out_shape=out_shape,
        scratch_shapes=[pltpu.VMEM((MAX_POOL_WINDOW, POOL_WIDTH), F32),
                        pltpu.VMEM((n_stage, INPROJ_SUB, LANES), F32),
                        pltpu.VMEM((n_stage, INPROJ_SUB, LANES), F32),
                        pltpu.VMEM((2, INPROJ_SUB, D_MODEL), BF16),
                        pltpu.VMEM((2, 4, INPROJ_SUB, LANES), F32),
                        pltpu.VMEM((INPROJ_SUB, POOL_WIDTH), BF16),
                        pltpu.VMEM((INPROJ_SUB // LANES, LANES, LANES), BF16)],
        compiler_params=_params(2),
        name="inproj",
    )(x, x, mod3, mod3, g_pre, w_in, pos4, pos4, inv_freq.reshape(ROT_HALF, 1), expand, wpool_blk, bpool,
      pscale)


def _attend(q, keys, vals, bias, head_masks):
    q_heads = jnp.concatenate([jnp.where(m, q, jnp.zeros_like(q)) for m in head_masks], axis=0)
    scores = lax.dot_general(q_heads, keys, (((1,), (1,)), ((), ())), preferred_element_type=F32)
    probs, lses = [], []
    for h in range(HEADS_PER_GROUP):
        s = scores[h * BLOCK:(h + 1) * BLOCK] + bias
        m_row = jnp.max(s, axis=-1, keepdims=True)
        p = jnp.exp(s - m_row)
        den = jnp.sum(p, axis=-1, keepdims=True)
        probs.append((p * (1.0 / den)).astype(BF16))
        lses.append(m_row + jnp.log(den))
    pv = jnp.dot(jnp.concatenate(probs, axis=0), vals, preferred_element_type=F32)
    out = pv[(HEADS_PER_GROUP - 1) * BLOCK:]
    lse_out = jnp.broadcast_to(lses[-1], (BLOCK, GROUP_WIDTH))
    for h in range(HEADS_PER_GROUP - 2, -1, -1):
        out = jnp.where(head_masks[h], pv[h * BLOCK:(h + 1) * BLOCK], out)
        lse_out = jnp.where(head_masks[h], lses[h], lse_out)
    return out, lse_out


def _attn_kernel(q0_ref, kp0_ref, kc0_ref, vp0_ref, vc0_ref,
                 q1_ref, kp1_ref, kc1_ref, vp1_ref, vc1_ref,
                 q2_ref, kp2_ref, kc2_ref, vp2_ref, vc2_ref, wout_ref, wup_ref, wdown_ref,
                 o_ref, wout_bf_ref, wup_bf_ref, wdown_bf_ref, ost_ref, lst_ref):
    for src_ref, dst_ref in ((wout_ref, wout_bf_ref), (wup_ref, wup_bf_ref), (wdown_ref, wdown_bf_ref)):
        dst_ref[...] = src_ref[...].astype(BF16)
    first_chunk = pl.program_id(1) == 0
    i = lax.broadcasted_iota(jnp.int32, (BLOCK, 2 * BLOCK), 0)
    j = lax.broadcasted_iota(jnp.int32, (BLOCK, 2 * BLOCK), 1)
    valid = (j >= i) & (j <= i + SPAN)
    bias = jnp.where(valid, 0.0, MASK_VALUE)
    has_prev = jnp.logical_not(first_chunk)
    bias_first = jnp.where(valid & ((j >= BLOCK) | has_prev), 0.0, MASK_VALUE)
    lane = lax.broadcasted_iota(jnp.int32, (BLOCK, GROUP_WIDTH), 1)
    head_masks = [(lane // HEAD_DIM) == h for h in range(HEADS_PER_GROUP)]
    groups = ((q0_ref, kp0_ref, kc0_ref, vp0_ref, vc0_ref),
              (q1_ref, kp1_ref, kc1_ref, vp1_ref, vc1_ref),
              (q2_ref, kp2_ref, kc2_ref, vp2_ref, vc2_ref))

    def first_tile(g, r):
        q_ref, kp_ref, kc_ref, vp_ref, vc_ref = groups[g]
        keys = jnp.concatenate([kp_ref[0, r], kc_ref[0, r, :BLOCK, :]], axis=0)
        vals = jnp.concatenate([vp_ref[0, r], vc_ref[0, r, :BLOCK, :]], axis=0)
        return _attend(q_ref[0, r, :BLOCK, :], keys, vals, bias_first, head_masks)

    def later_tile(g, r, n):
        q_ref, _, kc_ref, _, vc_ref = groups[g]
        start = (n - 1) * BLOCK
        both = pl.ds(start, 2 * BLOCK)
        return _attend(q_ref[0, r, pl.ds(start + BLOCK, BLOCK), :], kc_ref[0, r, both, :],
                       vc_ref[0, r, both, :], bias, head_masks)

    def token_rows(g, r, n):
        d = DILATIONS[g]
        start = n * (d * BLOCK) + r
        return pl.ds(start, BLOCK, stride=d) if d > 1 else pl.ds(start, BLOCK)

    def merged(rows, out, lse, half):
        cols = slice(half * LANES, (half + 1) * LANES)
        o_a, l_a, o_b, l_b = ost_ref[half, rows, :], lst_ref[half, rows, :], out[:, cols], lse[:, cols]
        top = jnp.maximum(l_a, l_b)
        w_a = jnp.exp(l_a - top)
        w_b = jnp.exp(l_b - top)
        total = w_a + w_b
        return (w_a * o_a + w_b * o_b) * (1.0 / total), top + jnp.log(total)

    def start_state(rows, out, lse):
        for half in range(HALVES):
            cols = slice(half * LANES, (half + 1) * LANES)
            ost_ref[half, rows, :] = out[:, cols]
            lst_ref[half, rows, :] = lse[:, cols]

    def fold_state(rows, out, lse):
        for half in range(HALVES):
            ost_ref[half, rows, :], lst_ref[half, rows, :] = merged(rows, out, lse, half)

    def finish(rows, out, lse):
        for half in range(HALVES):
            o_ref[0, rows, half * LANES:(half + 1) * LANES] = merged(rows, out, lse, half)[0].astype(BF16)

    blocks = [ATTN_CHUNK // (d * BLOCK) for d in DILATIONS]
    for r in range(DILATIONS[2]):
        start_state(token_rows(2, r, 0), *first_tile(2, r))
    for r in range(DILATIONS[1]):
        fold_state(token_rows(1, r, 0), *first_tile(1, r))
        for n in range(1, blocks[1]):
            fold_state(token_rows(1, r, n), *later_tile(1, r, n))
    finish(token_rows(0, 0, 0), *first_tile(0, 0))
    for n in range(1, blocks[0]):
        finish(token_rows(0, 0, n), *later_tile(0, 0, n))


def _attn_call(qkv, weights):
    batch = qkv[0][0].shape[0]
    seq = qkv[0][0].shape[2]
    chunks = seq // ATTN_CHUNK
    n_steps = batch * chunks
    in_specs, args = [], []
    for (q, k, v), d in zip(qkv, DILATIONS):
        rows = ATTN_CHUNK // d
        cur = pl.BlockSpec((1, d, rows, GROUP_WIDTH), lambda b, c: (b, 0, c, 0))
        blocks_per_chunk = rows // BLOCK
        prev = pl.BlockSpec((1, d, BLOCK, GROUP_WIDTH),
                            lambda b, c, n=blocks_per_chunk: (b, 0, jnp.maximum(c * n - 1, 0), 0))
        in_specs += [cur, prev, cur, prev, cur]
        args += [q, k, k, v, v]
    weight_specs = [pl.BlockSpec((w.shape[0] // n_steps, w.shape[1]), lambda b, c: (b * chunks + c, 0))
                    for w in weights]
    return pl.pallas_call(
        _attn_kernel,
        grid=(batch, chunks),
        in_specs=in_specs + weight_specs,
        out_specs=[pl.BlockSpec((1, ATTN_CHUNK, GROUP_WIDTH), lambda b, c: (b, c, 0))] + weight_specs,
        out_shape=[jax.ShapeDtypeStruct((batch, seq, GROUP_WIDTH), BF16)]
                  + [jax.ShapeDtypeStruct(w.shape, BF16) for w in weights],
        scratch_shapes=[pltpu.VMEM((HALVES, ATTN_CHUNK, LANES), F32),
                        pltpu.VMEM((HALVES, ATTN_CHUNK, LANES), F32)],
        compiler_params=_params(2),
        name="attn",
    )(*args, *weights)


def _ffn_kernel(x_ref, pool_ref, attn_ref, xn_ref, pooln_ref, attnn_ref, mod_ref, modn_ref,
                gpostm_ref, gpref_ref, gpostf_ref, wout_ref, wup_ref, cw_ref, cb_ref, wdown_ref,
                out_ref, act_ref, carry_ref, x1a_ref, x1b_ref, h2a_ref, h2b_ref, raw_ref):
    s = pl.program_id(1)
    tm = out_ref.shape[1]
    halo = carry_ref.shape[0]
    n_pieces = tm // FFN_PIECE

    @pl.when(s == 0)
    def _():
        carry_ref[...] = jnp.zeros_like(carry_ref)

    def prologue_piece(k, x1_dst, h2_dst, xs_ref, ps_ref, as_ref, ms_ref):
        rows = slice(k * FFN_PIECE, (k + 1) * FFN_PIECE)
        mixed = jnp.concatenate([ps_ref[0, rows, :], as_ref[0, rows, :]], axis=-1)
        y = jnp.dot(mixed, wout_ref[...], preferred_element_type=F32)
        x1 = xs_ref[0, rows, :] + ms_ref[0, 2:3, :] * (y * _rms_scale(y) * gpostm_ref[...])
        x1_dst[rows, :] = x1
        h2 = x1 * _rms_scale(x1) * gpref_ref[...] * (1.0 + ms_ref[0, 4:5, :]) + ms_ref[0, 3:4, :]
        h2_dst[rows, :] = h2.astype(BF16)

    def step(x1_cur, h2_cur, x1_next, h2_next):
        h2 = h2_cur[...]
        n_chunks = D_FF // FF_CHUNK
        for c in range(n_chunks):
            cols = slice(c * FF_CHUNK, (c + 1) * FF_CHUNK)
            vcols = slice(D_FF + c * FF_CHUNK, D_FF + (c + 1) * FF_CHUNK)
            raw_ref[c % 2, 0] = jnp.dot(h2, wup_ref[:, cols], preferred_element_type=F32)
            raw_ref[c % 2, 1] = jnp.dot(h2, wup_ref[:, vcols], preferred_element_type=F32)
            gate, val = raw_ref[c % 2, 0], raw_ref[c % 2, 1]
            ext = jnp.concatenate([carry_ref[:, cols], gate], axis=0)
            carry_ref[:, cols] = gate[tm - halo:, :]
            back1 = pltpu.roll(ext, 1, 0)[halo:, :]
            back2 = pltpu.roll(ext, 2, 0)[halo:, :]
            conv = (back2 * cw_ref[0:1, cols] + back1 * cw_ref[1:2, cols] + gate * cw_ref[2:3, cols]
                    + cb_ref[:, cols])
            act_ref[:, cols] = (jax.nn.gelu(conv, approximate=True) * val).astype(BF16)
        act = act_ref[...]
        y_parts = []
        for k in range(D_MODEL // DOWN_CHUNK):
            if k < n_pieces:
                prologue_piece(k, x1_next, h2_next, xn_ref, pooln_ref, attnn_ref, modn_ref)
            y_parts.append(jnp.dot(act, wdown_ref[:, k * DOWN_CHUNK:(k + 1) * DOWN_CHUNK],
                                   preferred_element_type=F32))
        y = jnp.concatenate(y_parts, axis=-1)
        out_ref[0] = x1_cur[...] + mod_ref[0, 5:6, :] * (y * _rms_scale(y) * gpostf_ref[...])

    @pl.when((pl.program_id(0) == 0) & (s == 0))
    def _():
        for k in range(n_pieces):
            prologue_piece(k, x1a_ref, h2a_ref, x_ref, pool_ref, attn_ref, mod_ref)

    @pl.when(s % 2 == 0)
    def _():
        step(x1a_ref, h2a_ref, x1b_ref, h2b_ref)

    @pl.when(s % 2 == 1)
    def _():
        step(x1b_ref, h2b_ref, x1a_ref, h2a_ref)


def _ffn_call(x, pool, attn, mod3, g_post_mix, g_pre_ffn, g_post_ffn, w_out, w_up, conv_w, conv_b, w_down):
    batch, seq, _ = x.shape
    tm = ROW_TILE
    steps = seq // tm
    assert steps % 2 == 0
    row = lambda b, s: (b, s, 0)
    const = lambda b, s: (0, 0)

    def nxt(b, s):
        return jnp.minimum(b + (s + 1) // steps, batch - 1), (s + 1) % steps, 0

    resident = dict(pipeline_mode=pl.Buffered(1))
    vec = pl.BlockSpec((1, D_MODEL), const)
    return pl.pallas_call(
        _ffn_kernel,
        grid=(batch, steps),
        in_specs=[pl.BlockSpec((1, tm, D_MODEL), row),
                  pl.BlockSpec((1, tm, GROUP_WIDTH), row),
                  pl.BlockSpec((1, tm, GROUP_WIDTH), row),
                  pl.BlockSpec((1, tm, D_MODEL), nxt),
                  pl.BlockSpec((1, tm, GROUP_WIDTH), nxt),
                  pl.BlockSpec((1, tm, GROUP_WIDTH), nxt),
                  pl.BlockSpec((1, N_MOD, D_MODEL), lambda b, s: (b, 0, 0)),
                  pl.BlockSpec((1, N_MOD, D_MODEL), lambda b, s: (nxt(b, s)[0], 0, 0)),
                  vec, vec, vec,
                  pl.BlockSpec((POOL_WIDTH + GROUP_WIDTH, D_MODEL), const, **resident),
                  pl.BlockSpec((D_MODEL, 2 * D_FF), const, **resident),
                  pl.BlockSpec((3, D_FF), const),
                  pl.BlockSpec((1, D_FF), const),
                  pl.BlockSpec((D_FF, D_MODEL), const, **resident)],
        out_specs=pl.BlockSpec((1, tm, D_MODEL), row),
        out_shape=jax.ShapeDtypeStruct((batch, seq, D_MODEL), F32),
        scratch_shapes=[pltpu.VMEM((tm, D_FF), BF16),
                        pltpu.VMEM((SUBLANES, D_FF), F32),
                        pltpu.VMEM((tm, D_MODEL), F32),
                        pltpu.VMEM((tm, D_MODEL), F32),
                        pltpu.VMEM((tm, D_MODEL), BF16),
                        pltpu.VMEM((tm, D_MODEL), BF16),
                        pltpu.VMEM((2, 2, tm, FF_CHUNK), F32)],
        compiler_params=_params(2),
        name="ffn",
    )(x, pool, attn, x, pool, attn, mod3, mod3, g_post_mix, g_pre_ffn, g_post_ffn, w_out, w_up, conv_w,
      conv_b, w_down)


def kernel(x, c, positions, w_ada, b_ada, g_pre_mix, g_post_mix, g_pre_ffn, g_post_ffn,
           w_in, w_pool, b_pool, pool_scale, w_out, w_up, conv_w, conv_b, w_down):
    depth = w_ada.shape[0]
    batch = x.shape[0]
    for l in range(depth):
        mod, w_in_bf = _mod_call(c, w_ada[l], b_ada[l], w_in[l])
        mod3 = mod.reshape(batch, N_MOD, D_MODEL)
        wpool_blk = jax.scipy.linalg.block_diag(*[w_pool[l, g] for g in range(len(POOL_WINDOWS))])
        pool, *qkv = _inproj_call(
            x, mod3, g_pre_mix[l].reshape(1, D_MODEL), w_in_bf, positions,
            wpool_blk.astype(BF16), b_pool[l].reshape(1, POOL_WIDTH), pool_scale[l].reshape(1, POOL_WIDTH))
        attn, w_out_bf, w_up_bf, w_down_bf = _attn_call([qkv[0:3], qkv[3:6], qkv[6:9]],
                                                         [w_out[l], w_up[l], w_down[l]])
        x = _ffn_call(x, pool, attn, mod3, g_post_mix[l].reshape(1, D_MODEL), g_pre_ffn[l].reshape(1, D_MODEL),
                      g_post_ffn[l].reshape(1, D_MODEL), w_out_bf, w_up_bf, conv_w[l],
                      conv_b[l].reshape(1, D_FF), w_down_bf)
    return x
```

```python
import numpy as np

import jax
import jax.numpy as jnp
from jax import lax
from jax.experimental import pallas as pl
from jax.experimental.pallas import tpu as pltpu

F32 = jnp.float32
BF16 = jnp.bfloat16

D_MODEL = 1024
HEAD_DIM = 64
POOL_WIDTH = 256
POOL_WINDOWS = (2, 4, 8, 16)
POOL_GROUP_DIM = 64
MAX_POOL_WINDOW = 16
ATTN_WIDTH = 768
GROUP_WIDTH = 256
HEADS_PER_GROUP = 4
DILATIONS = (1, 4, 16)
DESTRIDE_STEP = 4
SPAN = 128
BLOCK = 128
IN_PROJ_WIDTH = 2560
ROT_DIM = 16
ROT_HALF = 8
ROPE_THETA = 500000.0
D_FF = 2816
NORM_EPS = 1e-6
N_MOD = 6
MASK_VALUE = -1e30

LANES = 128
SUBLANES = 8
VMEM_LIMIT_BYTES = 56 * 1024 * 1024

FFN_TILE = 1024
FFN_SUB = 512
FFN_PIECE = 128
DOWN_CHUNK = 256
INPROJ_TILE = 1024
INPROJ_SUB = 512
PIECE = 64
FF_CHUNK = 256
ATTN_CHUNK = BLOCK * DILATIONS[-1]
HALVES = GROUP_WIDTH // LANES
N_SPLIT = 3


def _params(n_axes, vmem=VMEM_LIMIT_BYTES):
    return pltpu.CompilerParams(dimension_semantics=("arbitrary",) * n_axes, vmem_limit_bytes=vmem)


def _rms_scale(v):
    return lax.rsqrt(jnp.mean(v * v, axis=-1, keepdims=True) + NORM_EPS)


def _mod_kernel(c_ref, w_ref, b_ref, win_ref, o_ref, win_bf_ref):
    c = c_ref[...]
    c_act = c * jax.nn.sigmoid(c)
    o_ref[...] = jnp.dot(c_act.astype(BF16), w_ref[...].astype(BF16),
                         preferred_element_type=F32) + b_ref[...]
    win_bf_ref[...] = win_ref[...].astype(BF16)


def _mod_call(c, w_ada, b_ada, w_in):
    batch = c.shape[0]
    n = w_ada.shape[1]
    tn = 768
    steps = n // tn
    slab = pl.BlockSpec((w_in.shape[0] // steps, w_in.shape[1]), lambda j: (j, 0))
    return pl.pallas_call(
        _mod_kernel,
        grid=(steps,),
        in_specs=[pl.BlockSpec((batch, D_MODEL), lambda j: (0, 0)),
                  pl.BlockSpec((D_MODEL, tn), lambda j: (0, j)),
                  pl.BlockSpec((1, tn), lambda j: (0, j)),
                  slab],
        out_specs=[pl.BlockSpec((batch, tn), lambda j: (0, j)), slab],
        out_shape=[jax.ShapeDtypeStruct((batch, n), F32), jax.ShapeDtypeStruct(w_in.shape, BF16)],
        compiler_params=_params(1),
        name="mod",
    )(c, w_ada, b_ada.reshape(1, n), w_in)


def _rope_expansion():
    e = np.zeros((LANES, 2 * LANES), np.float32)
    ones_row = 2 * N_SPLIT * ROT_HALF
    for lane in range(LANES):
        in_head = lane % HEAD_DIM
        if in_head >= ROT_DIM:
            e[ones_row, lane] = 1.0
            continue
        sign = -1.0 if in_head < ROT_HALF else 1.0
        for t in range(N_SPLIT):
            e[t * ROT_HALF + in_head % ROT_HALF, lane] = 1.0
            e[(N_SPLIT + t) * ROT_HALF + in_head % ROT_HALF, LANES + lane] = sign
    return e


def _rope_terms(pos_row, freq_col):
    ang = freq_col * pos_row
    terms = []
    for table in (jnp.cos(ang), jnp.sin(ang)):
        rest = table
        for _ in range(N_SPLIT):
            part = rest.astype(BF16).astype(F32)
            terms.append(part)
            rest = rest - part
    terms.append(jnp.ones((SUBLANES, LANES), F32))
    pad = jnp.zeros((LANES - SUBLANES * len(terms), LANES), F32)
    return jnp.concatenate(terms + [pad], axis=0).T.astype(BF16)


def _rope_expand(terms, expand):
    both = jnp.dot(terms, expand, preferred_element_type=F32)
    return both[:, :LANES], both[:, LANES:]


def _inproj_kernel(x_ref, xn_ref, mod_ref, modn_ref, g_ref, w_ref, pos_ref, posn_ref, freq_ref, expand_ref,
                   wpool_ref, bpool_ref, pscale_ref,
                   pool_ref, q0_ref, k0_ref, v0_ref, q1_ref, k1_ref, v1_ref, q2_ref, k2_ref, v2_ref,
                   carry_ref, stage_ref, mid_ref, h_ref, tab_ref, mixed_ref, terms_ref):
    s = pl.program_id(1)
    tm = x_ref.shape[1]
    sub = INPROJ_SUB
    q_scale = HEAD_DIM ** -0.5
    first_half = (lax.broadcasted_iota(jnp.int32, (sub, LANES), 1) % HEAD_DIM) < ROT_HALF
    low_group = lax.broadcasted_iota(jnp.int32, (sub + MAX_POOL_WINDOW, LANES), 1) < POOL_GROUP_DIM
    low_group_head = lax.broadcasted_iota(jnp.int32, (MAX_POOL_WINDOW, LANES), 1) < POOL_GROUP_DIM
    low_group_row = lax.broadcasted_iota(jnp.int32, (1, LANES), 1) < POOL_GROUP_DIM
    out_refs = ((q0_ref, k0_ref, v0_ref), (q1_ref, k1_ref, v1_ref), (q2_ref, k2_ref, v2_ref))

    @pl.when(s == 0)
    def _():
        carry_ref[...] = jnp.zeros_like(carry_ref)

    def rope(t, cos, sin):
        partner = jnp.where(first_half, pltpu.roll(t, LANES - ROT_HALF, 1), pltpu.roll(t, ROT_HALF, 1))
        return t * cos + partner * sin

    def emit(out_ref, value, d, half, a, slot):
        cols = slice(half * LANES, (half + 1) * LANES)
        rows_out = sub // d
        out_rows = slice(a * rows_out, (a + 1) * rows_out)
        if d == 1:
            out_ref[0, 0, out_rows, cols] = value.astype(BF16)
            return
        stage_ref[slot] = value
        step = DESTRIDE_STEP
        if d == step:
            for r in range(d):
                picked = stage_ref[slot, pl.ds(r, rows_out, stride=step), :]
                out_ref[0, r, out_rows, cols] = picked.astype(BF16)
            return
        part = sub // step
        for r_lo in range(step):
            mid_ref[slot, r_lo * part:(r_lo + 1) * part, :] = stage_ref[slot, pl.ds(r_lo, part, stride=step), :]
        for r_lo in range(step):
            for r_hi in range(step):
                r = r_lo + step * r_hi
                picked = mid_ref[slot, pl.ds(r_lo * part + r_hi, rows_out, stride=step), :]
                out_ref[0, r, out_rows, cols] = picked.astype(BF16)

    def pool_mix(u, row0):
        ext = jnp.concatenate([carry_ref[...], u], axis=0)
        carry_ref[...] = u[sub - MAX_POOL_WINDOW:, :]
        head = MAX_POOL_WINDOW
        head_row = lax.broadcasted_iota(jnp.int32, (head, LANES), 0) + row0 + 1
        mixed = []
        for half in range(POOL_WIDTH // LANES):
            w_a, w_b = POOL_WINDOWS[2 * half], POOL_WINDOWS[2 * half + 1]
            acc = ext[:, half * LANES:(half + 1) * LANES]
            width = 1
            while width < w_a:
                acc = acc + pltpu.roll(acc, width, 0)
                width *= 2
            sum_a = acc
            while width < w_b:
                acc = acc + pltpu.roll(acc, width, 0)
                width *= 2
            win = jnp.where(low_group, sum_a, acc)[MAX_POOL_WINDOW:, :]
            window = jnp.where(low_group_head, w_a, w_b)
            count = jnp.minimum(head_row, window).astype(F32)
            inv_window = jnp.where(low_group_row, 1.0 / w_a, 1.0 / w_b)
            mean = jnp.concatenate([win[:head] / count, win[head:] * inv_window], axis=0)
            mixed.append(mean - u[:, half * LANES:(half + 1) * LANES])
        mixed_ref[...] = jnp.concatenate(mixed, axis=1).astype(BF16)

    def pool_finish(rows):
        y = jnp.dot(mixed_ref[...], wpool_ref[...], preferred_element_type=F32) + bpool_ref[...]
        pool_ref[0, rows, :] = (y * pscale_ref[...]).astype(BF16)

    def prologue_piece(k, dst, xsrc_ref, row0, psrc_ref, prow0, gain, shift):
        rows = slice(k * PIECE, (k + 1) * PIECE)
        x = xsrc_ref[0, row0 + k * PIECE:row0 + (k + 1) * PIECE, :]
        h_ref[dst, rows, :] = (x * _rms_scale(x) * gain + shift).astype(BF16)
        if 1 <= k <= sub // LANES:
            cos, sin = _rope_expand(terms_ref[k - 1], expand_ref[...])
            for i, table in enumerate((cos, sin, cos * q_scale, sin * q_scale)):
                tab_ref[dst, i, (k - 1) * LANES:k * LANES, :] = table
        if k < sub // LANES:
            pos = psrc_ref[0, 0, prow0 + k:prow0 + k + 1, :].astype(F32)
            terms_ref[k] = _rope_terms(pos, freq_ref[...])

    n_sub = tm // sub
    n_pieces = sub // PIECE
    shift = mod_ref[0, 0:1, :]
    gain = g_ref[...] * (1.0 + mod_ref[0, 1:2, :])
    shift_next = modn_ref[0, 0:1, :]
    gain_next = g_ref[...] * (1.0 + modn_ref[0, 1:2, :])

    @pl.when((pl.program_id(0) == 0) & (s == 0))
    def _():
        for k in range(n_pieces):
            prologue_piece(k, 0, x_ref, 0, pos_ref, 0, gain, shift)

    chunks = [(g, kind) for g in range(len(DILATIONS)) for kind in range(3)]
    chunks.remove((0, 2))
    chunks.append((None, None))
    chunks.append((0, 2))
    slots = {}
    for g, d in enumerate(DILATIONS):
        if d > 1:
            for kind in range(3):
                for half in range(HALVES):
                    slots[g, kind, half] = len(slots)

    for a in range(n_sub):
        rows = slice(a * sub, (a + 1) * sub)
        h = h_ref[a % 2]
        cos, sin, cos_q, sin_q = (tab_ref[a % 2, i] for i in range(4))
        for ci, (g, kind) in enumerate(chunks):
            if g is None:
                pool_mix(jnp.dot(h, w_ref[:, :POOL_WIDTH], preferred_element_type=F32), s * tm + a * sub)
            else:
                lo = POOL_WIDTH + kind * ATTN_WIDTH + g * GROUP_WIDTH
                proj = jnp.dot(h, w_ref[:, lo:lo + GROUP_WIDTH], preferred_element_type=F32)
                for half in range(HALVES):
                    value = proj[:, half * LANES:(half + 1) * LANES]
                    if kind == 0:
                        value = rope(value, cos_q, sin_q)
                    elif kind == 1:
                        value = rope(value, cos, sin)
                    emit(out_refs[g][kind], value, DILATIONS[g], half, a, slots.get((g, kind, half)))
            if ci > 0 and chunks[ci - 1][0] is None:
                pool_finish(rows)
            if ci < n_pieces:
                if a + 1 < n_sub:
                    prologue_piece(ci, (a + 1) % 2, x_ref, (a + 1) * sub, pos_ref, (a + 1) * (sub // LANES),
                                   gain, shift)
                else:
                    prologue_piece(ci, (a + 1) % 2, xn_ref, 0, posn_ref, 0, gain_next, shift_next)


def _inproj_call(x, mod3, g_pre, w_in, positions, wpool_blk, bpool, pscale):
    batch, seq, _ = x.shape
    tm = INPROJ_TILE
    row = lambda b, s: (b, s, 0)
    const = lambda b, s: (0, 0)
    resident = dict(pipeline_mode=pl.Buffered(1))
    inv_freq = ROPE_THETA ** (-jnp.arange(0, ROT_DIM, 2, dtype=F32) / ROT_DIM)
    pos4 = positions.reshape(batch, seq // tm, tm // LANES, LANES)
    expand = jnp.asarray(_rope_expansion(), BF16)
    steps = seq // tm

    def next_step(b, s):
        wrap = (s + 1) // steps
        return jnp.minimum(b + wrap, batch - 1), (s + 1) % steps

    out_specs = [pl.BlockSpec((1, tm, GROUP_WIDTH), row)]
    out_shape = [jax.ShapeDtypeStruct((batch, seq, GROUP_WIDTH), BF16)]
    n_stage = 0
    for d in DILATIONS:
        out_specs += [pl.BlockSpec((1, d, tm // d, GROUP_WIDTH), lambda b, s: (b, 0, s, 0))] * 3
        out_shape += [jax.ShapeDtypeStruct((batch, d, seq // d, GROUP_WIDTH), BF16)] * 3
        n_stage += 3 * HALVES if d > 1 else 0
    return pl.pallas_call(
        _inproj_kernel,
        grid=(batch, seq // tm),
        in_specs=[pl.BlockSpec((1, tm, D_MODEL), row),
                  pl.BlockSpec((1, INPROJ_SUB, D_MODEL),
                               lambda b, s: (next_step(b, s)[0], next_step(b, s)[1] * (tm // INPROJ_SUB), 0)),
                  pl.BlockSpec((1, N_MOD, D_MODEL), lambda b, s: (b, 0, 0)),
                  pl.BlockSpec((1, N_MOD, D_MODEL), lambda b, s: (next_step(b, s)[0], 0, 0)),
                  pl.BlockSpec((1, D_MODEL), const),
                  pl.BlockSpec((D_MODEL, IN_PROJ_WIDTH), const, **resident),
                  pl.BlockSpec((1, 1, tm // LANES, LANES), lambda b, s: (b, s, 0, 0)),
                  pl.BlockSpec((1, 1, tm // LANES, LANES), lambda b, s: (*next_step(b, s), 0, 0)),
                  pl.BlockSpec((ROT_HALF, 1), const),
                  pl.BlockSpec((LANES, 2 * LANES), const),
                  pl.BlockSpec((POOL_WIDTH, POOL_WIDTH), const),
                  pl.BlockSpec((1, POOL_WIDTH), const),
                  pl.BlockSpec((1, POOL_WIDTH), const)],
        out_specs=out_specs,
        out_shape=out_shape,
        scratch_shapes=[pltpu.VMEM((MAX_POOL_WINDOW, POOL_WIDTH), F32),
                        pltpu.VMEM((n_stage, INPROJ_SUB, LANES), F32),
                        pltpu.VMEM((n_stage, INPROJ_SUB, LANES), F32),
                        pltpu.VMEM((2, INPROJ_SUB, D_MODEL), BF16),
                        pltpu.VMEM((2, 4, INPROJ_SUB, LANES), F32),
                        pltpu.VMEM((INPROJ_SUB, POOL_WIDTH), BF16),
                        pltpu.VMEM((INPROJ_SUB // LANES, LANES, LANES), BF16)],
        compiler_params=_params(2),
        name="inproj",
    )(x, x, mod3, mod3, g_pre, w_in, pos4, pos4, inv_freq.reshape(ROT_HALF, 1), expand, wpool_blk, bpool,
      pscale)


def _attend(q, keys, vals, bias, head_masks):
    q_heads = jnp.concatenate([jnp.where(m, q, jnp.zeros_like(q)) for m in head_masks], axis=0)
    scores = lax.dot_general(q_heads, keys, (((1,), (1,)), ((), ())), preferred_element_type=F32)
    probs, lses = [], []
    for h in range(HEADS_PER_GROUP):
        s = scores[h * BLOCK:(h + 1) * BLOCK] + bias
        m_row = jnp.max(s, axis=-1, keepdims=True)
        p = jnp.exp(s - m_row)
        den = jnp.sum(p, axis=-1, keepdims=True)
        probs.append((p * (1.0 / den)).astype(BF16))
        lses.append(m_row + jnp.log(den))
    pv = jnp.dot(jnp.concatenate(probs, axis=0), vals, preferred_element_type=F32)
    out = pv[(HEADS_PER_GROUP - 1) * BLOCK:]
    lse_out = jnp.broadcast_to(lses[-1], (BLOCK, GROUP_WIDTH))
    for h in range(HEADS_PER_GROUP - 2, -1, -1):
        out = jnp.where(head_masks[h], pv[h * BLOCK:(h + 1) * BLOCK], out)
        lse_out = jnp.where(head_masks[h], lses[h], lse_out)
    return out, lse_out


def _attn_kernel(q0_ref, kp0_ref, kc0_ref, vp0_ref, vc0_ref,
                 q1_ref, kp1_ref, kc1_ref, vp1_ref, vc1_ref,
                 q2_ref, kp2_ref, kc2_ref, vp2_ref, vc2_ref, wout_ref, wup_ref, wdown_ref,
                 o_ref, wout_bf_ref, wup_bf_ref, wdown_bf_ref, ost_ref, lst_ref):
    for src_ref, dst_ref in ((wout_ref, wout_bf_ref), (wup_ref, wup_bf_ref), (wdown_ref, wdown_bf_ref)):
        dst_ref[...] = src_ref[...].astype(BF16)
    first_chunk = pl.program_id(1) == 0
    i = lax.broadcasted_iota(jnp.int32, (BLOCK, 2 * BLOCK), 0)
    j = lax.broadcasted_iota(jnp.int32, (BLOCK, 2 * BLOCK), 1)
    valid = (j >= i) & (j <= i + SPAN)
    bias = jnp.where(valid, 0.0, MASK_VALUE)
    has_prev = jnp.logical_not(first_chunk)
    bias_first = jnp.where(valid & ((j >= BLOCK) | has_prev), 0.0, MASK_VALUE)
    lane = lax.broadcasted_iota(jnp.int32, (BLOCK, GROUP_WIDTH), 1)
    head_masks = [(lane // HEAD_DIM) == h for h in range(HEADS_PER_GROUP)]
    groups = ((q0_ref, kp0_ref, kc0_ref, vp0_ref, vc0_ref),
              (q1_ref, kp1_ref, kc1_ref, vp1_ref, vc1_ref),
              (q2_ref, kp2_ref, kc2_ref, vp2_ref, vc2_ref))

    def first_tile(g, r):
        q_ref, kp_ref, kc_ref, vp_ref, vc_ref = groups[g]
        keys = jnp.concatenate([kp_ref[0, r], kc_ref[0, r, :BLOCK, :]], axis=0)
        vals = jnp.concatenate([vp_ref[0, r], vc_ref[0, r, :BLOCK, :]], axis=0)
        return _attend(q_ref[0, r, :BLOCK, :], keys, vals, bias_first, head_masks)

    def later_tile(g, r, n):
        q_ref, _, kc_ref, _, vc_ref = groups[g]
        start = (n - 1) * BLOCK
        both = pl.ds(start, 2 * BLOCK)
        return _attend(q_ref[0, r, pl.ds(start + BLOCK, BLOCK), :], kc_ref[0, r, both, :],
                       vc_ref[0, r, both, :], bias, head_masks)

    def token_rows(g, r, n):
        d = DILATIONS[g]
        start = n * (d * BLOCK) + r
        return pl.ds(start, BLOCK, stride=d) if d > 1 else pl.ds(start, BLOCK)

    def merged(rows, out, lse, half):
        cols = slice(half * LANES, (half + 1) * LANES)
        o_a, l_a, o_b, l_b = ost_ref[half, rows, :], lst_ref[half, rows, :], out[:, cols], lse[:, cols]
        top = jnp.maximum(l_a, l_b)
        w_a = jnp.exp(l_a - top)
        w_b = jnp.exp(l_b - top)
        total = w_a + w_b
        return (w_a * o_a + w_b * o_b) * (1.0 / total), top + jnp.log(total)

    def start_state(rows, out, lse):
        for half in range(HALVES):
            cols = slice(half * LANES, (half + 1) * LANES)
            ost_ref[half, rows, :] = out[:, cols]
            lst_ref[half, rows, :] = lse[:, cols]

    def fold_state(rows, out, lse):
        for half in range(HALVES):
            ost_ref[half, rows, :], lst_ref[half, rows, :] = merged(rows, out, lse, half)

    def finish(rows, out, lse):
        for half in range(HALVES):
            o_ref[0, rows, half * LANES:(half + 1) * LANES] = merged(rows, out, lse, half)[0].astype(BF16)

    blocks = [ATTN_CHUNK // (d * BLOCK) for d in DILATIONS]
    for r in range(DILATIONS[2]):
        start_state(token_rows(2, r, 0), *first_tile(2, r))
    for r in range(DILATIONS[1]):
        fold_state(token_rows(1, r, 0), *first_tile(1, r))
        for n in range(1, blocks[1]):
            fold_state(token_rows(1, r, n), *later_tile(1, r, n))
    finish(token_rows(0, 0, 0), *first_tile(0, 0))
    for n in range(1, blocks[0]):
        finish(token_rows(0, 0, n), *later_tile(0, 0, n))


def _attn_call(qkv, weights):
    batch = qkv[0][0].shape[0]
    seq = qkv[0][0].shape[2]
    chunks = seq // ATTN_CHUNK
    n_steps = batch * chunks
    in_specs, args = [], []
    for (q, k, v), d in zip(qkv, DILATIONS):
        rows = ATTN_CHUNK // d
        cur = pl.BlockSpec((1, d, rows, GROUP_WIDTH), lambda b, c: (b, 0, c, 0))
        blocks_per_chunk = rows // BLOCK
        prev = pl.BlockSpec((1, d, BLOCK, GROUP_WIDTH),
                            lambda b, c, n=blocks_per_chunk: (b, 0, jnp.maximum(c * n - 1, 0), 0))
        in_specs += [cur, prev, cur, prev, cur]
        args += [q, k, k, v, v]
    weight_specs = [pl.BlockSpec((w.shape[0] // n_steps, w.shape[1]), lambda b, c: (b * chunks + c, 0))
                    for w in weights]
    return pl.pallas_call(
        _attn_kernel,
        grid=(batch, chunks),
        in_specs=in_specs + weight_specs,
        out_specs=[pl.BlockSpec((1, ATTN_CHUNK, GROUP_WIDTH), lambda b, c: (b, c, 0))] + weight_specs,
        out_shape=[jax.ShapeDtypeStruct((batch, seq, GROUP_WIDTH), BF16)]
                  + [jax.ShapeDtypeStruct(w.shape, BF16) for w in weights],
        scratch_shapes=[pltpu.VMEM((HALVES, ATTN_CHUNK, LANES), F32),
                        pltpu.VMEM((HALVES, ATTN_CHUNK, LANES), F32)],
        compiler_params=_params(2),
        name="attn",
    )(*args, *weights)


def _ffn_kernel(x_ref, pool_ref, attn_ref, xn_ref, pooln_ref, attnn_ref, mod_ref, modn_ref,
                gpostm_ref, gpref_ref, gpostf_ref, wout_ref, wup_ref, cw_ref, cb_ref, wdown_ref,
                out_ref, act_ref, carry_ref, x1a_ref, x1b_ref, h2a_ref, h2b_ref, raw_ref):
    s = pl.program_id(1)
    sub = FFN_SUB
    n_sub = out_ref.shape[1] // sub
    halo = carry_ref.shape[0]
    n_pieces = sub // FFN_PIECE
    slots = ((x1a_ref, h2a_ref), (x1b_ref, h2b_ref))
    assert n_sub % len(slots) == 0

    @pl.when(s == 0)
    def _():
        carry_ref[...] = jnp.zeros_like(carry_ref)

    def prologue_piece(k, x1_dst, h2_dst, xs_ref, ps_ref, as_ref, ms_ref, row0):
        rows = slice(k * FFN_PIECE, (k + 1) * FFN_PIECE)
        src = slice(row0 + k * FFN_PIECE, row0 + (k + 1) * FFN_PIECE)
        mixed = jnp.concatenate([ps_ref[0, src, :], as_ref[0, src, :]], axis=-1)
        y = jnp.dot(mixed, wout_ref[...], preferred_element_type=F32)
        x1 = xs_ref[0, src, :] + ms_ref[0, 2:3, :] * (y * _rms_scale(y) * gpostm_ref[...])
        x1_dst[rows, :] = x1
        h2 = x1 * _rms_scale(x1) * gpref_ref[...] * (1.0 + ms_ref[0, 4:5, :]) + ms_ref[0, 3:4, :]
        h2_dst[rows, :] = h2.astype(BF16)

    def sub_tile(a):
        x1_cur, h2_cur = slots[a % 2]
        x1_next, h2_next = slots[(a + 1) % 2]
        if a + 1 < n_sub:
            upcoming = (x_ref, pool_ref, attn_ref, mod_ref, (a + 1) * sub)
        else:
            upcoming = (xn_ref, pooln_ref, attnn_ref, modn_ref, 0)
        h2 = h2_cur[...]
        n_chunks = D_FF // FF_CHUNK
        for c in range(n_chunks):
            cols = slice(c * FF_CHUNK, (c + 1) * FF_CHUNK)
            vcols = slice(D_FF + c * FF_CHUNK, D_FF + (c + 1) * FF_CHUNK)
            raw_ref[c % 2, 0] = jnp.dot(h2, wup_ref[:, cols], preferred_element_type=F32)
            raw_ref[c % 2, 1] = jnp.dot(h2, wup_ref[:, vcols], preferred_element_type=F32)
            gate, val = raw_ref[c % 2, 0], raw_ref[c % 2, 1]
            ext = jnp.concatenate([carry_ref[:, cols], gate], axis=0)
            carry_ref[:, cols] = gate[sub - halo:, :]
            back1 = pltpu.roll(ext, 1, 0)[halo:, :]
            back2 = pltpu.roll(ext, 2, 0)[halo:, :]
            conv = (back2 * cw_ref[0:1, cols] + back1 * cw_ref[1:2, cols] + gate * cw_ref[2:3, cols]
                    + cb_ref[:, cols])
            act_ref[:, cols] = (jax.nn.gelu(conv, approximate=True) * val).astype(BF16)
        act = act_ref[...]
        y_parts = []
        for k in range(D_MODEL // DOWN_CHUNK):
            if k < n_pieces:
                prologue_piece(k, x1_next, h2_next, *upcoming)
            y_parts.append(jnp.dot(act, wdown_ref[:, k * DOWN_CHUNK:(k + 1) * DOWN_CHUNK],
                                   preferred_element_type=F32))
        y = jnp.concatenate(y_parts, axis=-1)
        out_ref[0, a * sub:(a + 1) * sub, :] = (
            x1_cur[...] + mod_ref[0, 5:6, :] * (y * _rms_scale(y) * gpostf_ref[...]))

    @pl.when((pl.program_id(0) == 0) & (s == 0))
    def _():
        for k in range(n_pieces):
            prologue_piece(k, x1a_ref, h2a_ref, x_ref, pool_ref, attn_ref, mod_ref, 0)

    for a in range(n_sub):
        sub_tile(a)


def _ffn_call(x, pool, attn, mod3, g_post_mix, g_pre_ffn, g_post_ffn, w_out, w_up, conv_w, conv_b, w_down):
    batch, seq, _ = x.shape
    tm, sub = FFN_TILE, FFN_SUB
    steps = seq // tm
    row = lambda b, s: (b, s, 0)
    const = lambda b, s: (0, 0)

    def nxt(b, s):
        return jnp.minimum(b + (s + 1) // steps, batch - 1), ((s + 1) % steps) * (tm // sub), 0

    resident = dict(pipeline_mode=pl.Buffered(1))
    vec = pl.BlockSpec((1, D_MODEL), const)
    return pl.pallas_call(
        _ffn_kernel,
        grid=(batch, steps),
        in_specs=[pl.BlockSpec((1, tm, D_MODEL), row),
                  pl.BlockSpec((1, tm, GROUP_WIDTH), row),
                  pl.BlockSpec((1, tm, GROUP_WIDTH), row),
                  pl.BlockSpec((1, sub, D_MODEL), nxt),
                  pl.BlockSpec((1, sub, GROUP_WIDTH), nxt),
                  pl.BlockSpec((1, sub, GROUP_WIDTH), nxt),
                  pl.BlockSpec((1, N_MOD, D_MODEL), lambda b, s: (b, 0, 0)),
                  pl.BlockSpec((1, N_MOD, D_MODEL), lambda b, s: (nxt(b, s)[0], 0, 0)),
                  vec, vec, vec,
                  pl.BlockSpec((POOL_WIDTH + GROUP_WIDTH, D_MODEL), const, **resident),
                  pl.BlockSpec((D_MODEL, 2 * D_FF), const, **resident),
                  pl.BlockSpec((3, D_FF), const),
                  pl.BlockSpec((1, D_FF), const),
                  pl.BlockSpec((D_FF, D_MODEL), const, **resident)],
        out_specs=pl.BlockSpec((1, tm, D_MODEL), row),
        out_shape=jax.ShapeDtypeStruct((batch, seq, D_MODEL), F32),
        scratch_shapes=[pltpu.VMEM((sub, D_FF), BF16),
                        pltpu.VMEM((SUBLANES, D_FF), F32),
                        pltpu.VMEM((sub, D_MODEL), F32),
                        pltpu.VMEM((sub, D_MODEL), F32),
                        pltpu.VMEM((sub, D_MODEL), BF16),
                        pltpu.VMEM((sub, D_MODEL), BF16),
                        pltpu.VMEM((2, 2, sub, FF_CHUNK), F32)],
        compiler_params=_params(2),
        name="ffn",
    )(x, pool, attn, x, pool, attn, mod3, mod3, g_post_mix, g_pre_ffn, g_post_ffn, w_out, w_up, conv_w,
      conv_b, w_down)


def kernel(x, c, positions, w_ada, b_ada, g_pre_mix, g_post_mix, g_pre_ffn, g_post_ffn,
           w_in, w_pool, b_pool, pool_scale, w_out, w_up, conv_w, conv_b, w_down):
    depth = w_ada.shape[0]
    batch = x.shape[0]
    for l in range(depth):
        mod, w_in_bf = _mod_call(c, w_ada[l], b_ada[l], w_in[l])
        mod3 = mod.reshape(batch, N_MOD, D_MODEL)
        wpool_blk = jax.scipy.linalg.block_diag(*[w_pool[l, g] for g in range(len(POOL_WINDOWS))])
        pool, *qkv = _inproj_call(
            x, mod3, g_pre_mix[l].reshape(1, D_MODEL), w_in_bf, positions,
            wpool_blk.astype(BF16), b_pool[l].reshape(1, POOL_WIDTH), pool_scale[l].reshape(1, POOL_WIDTH))
        attn, w_out_bf, w_up_bf, w_down_bf = _attn_call([qkv[0:3], qkv[3:6], qkv[6:9]],
                                                         [w_out[l], w_up[l], w_down[l]])
        x = _ffn_call(x, pool, attn, mod3, g_post_mix[l].reshape(1, D_MODEL), g_pre_ffn[l].reshape(1, D_MODEL),
                      g_post_ffn[l].reshape(1, D_MODEL), w_out_bf, w_up_bf, conv_w[l],
                      conv_b[l].reshape(1, D_FF), w_down_bf)
    return x
```
